```python
import math
import jax, jax.numpy as jnp
from jax import lax
import numpy as np

D_MODEL = 1024
BATCH = 4
SEQ = 4096
DEPTH = 2
DEC_BATCH = 128
DEC_SEQ = 4
PAST_LEN = 16384
PAGE_SIZE = 128

GDN_HEADS = 4
GDN_DK = 128
GDN_DV = 128
GDN_CHUNK = 64
CONV_W = 4
SWA_HEADS = 8
SWA_KV_HEADS = 2
SWA_HEAD_DIM = 64
SWA_GROUP = SWA_HEADS // SWA_KV_HEADS
WINDOW = 128
Q_BLOCK = 128
GDN_QK_W = GDN_HEADS * GDN_DK
GDN_V_W = GDN_HEADS * GDN_DV
CONV_DIM = 2 * GDN_QK_W + GDN_V_W
SWA_Q_W = SWA_HEADS * SWA_HEAD_DIM
SWA_KV_W = SWA_KV_HEADS * SWA_HEAD_DIM
MIX_WIDTH = GDN_V_W + SWA_Q_W
IN_DIM = CONV_DIM + GDN_V_W + 2 * GDN_HEADS + SWA_Q_W + 2 * SWA_KV_W + 2 * D_MODEL
D_FF = 2816
N_EXPERTS = 8
TOP_K = 2
D_FF_EXPERT = 3584
N_DENSE = (DEPTH + 1) // 2
N_MOE = DEPTH // 2
EPS = 1e-6

kernel_name = 'hybrid_gdn_swa_sink_decoder_step'


def rms_norm(x, g):
    xf = x.astype(jnp.float32)
    y = xf * lax.rsqrt(jnp.mean(xf * xf, axis=-1, keepdims=True) + EPS)
    return (y * g.astype(jnp.float32)).astype(x.dtype)


def l2_norm(x):
    xf = x.astype(jnp.float32)
    return xf * lax.rsqrt(jnp.sum(xf * xf, axis=-1, keepdims=True) + EPS)


def causal_conv(x, prev, w):
    L = x.shape[1]
    xp = jnp.concatenate([prev.astype(x.dtype), x], axis=1)
    y = xp[:, 0:L] * w[0]
    for i in range(1, CONV_W):
        y = y + xp[:, i:i + L] * w[i]
    return jax.nn.silu(y), xp[:, -(CONV_W - 1):]


def gated_delta_rule(q, k, v, g, beta, S0):
    B, L, H, dk = q.shape
    dv = v.shape[-1]
    C = GDN_CHUNK if L % GDN_CHUNK == 0 else L
    N = L // C

    def chunks(t):
        t = t.reshape((B, N, C, H) + t.shape[3:])
        return jnp.moveaxis(t, 3, 1)

    q, k, v, g, beta = chunks(q), chunks(k), chunks(v), chunks(g), chunks(beta)
    G = jnp.cumsum(g, axis=-1)
    idx = jnp.arange(C)
    incl = idx[:, None] >= idx[None, :]
    strict = idx[:, None] > idx[None, :]
    decay = jnp.exp(jnp.where(incl, G[..., :, None] - G[..., None, :], -jnp.inf))
    kb = k * beta[..., None]
    lmat = jnp.where(strict, jnp.einsum('bhncd,bhnsd->bhncs', kb, k) * decay, 0.0)
    eye = jnp.eye(C, dtype=jnp.float32)
    rhs = jnp.concatenate([v * beta[..., None], kb * jnp.exp(G)[..., None]], axis=-1)
    X = lax.linalg.triangular_solve(lmat + eye, rhs, left_side=True, lower=True)
    u, w = X[..., :dv], X[..., dv:]
    qk = jnp.einsum('bhncd,bhnsd->bhncs', q, k) * decay
    g_last = G[..., -1]
    q_dec = q * jnp.exp(G)[..., None]
    k_dec = k * jnp.exp(g_last[..., None] - G)[..., None]

    def step(S, xs):
        u_n, w_n, qd_n, kd_n, qk_n, gl_n = xs
        v_new = u_n - jnp.einsum('bhcd,bhde->bhce', w_n, S)
        o_n = jnp.einsum('bhcd,bhde->bhce', qd_n, S) + jnp.einsum('bhcs,bhse->bhce', qk_n, v_new)
        S = jnp.exp(gl_n)[..., None, None] * S + jnp.einsum('bhcd,bhce->bhde', kd_n, v_new)
        return S, o_n

    xs = tuple(jnp.moveaxis(t, 2, 0) for t in (u, w, q_dec, k_dec, qk, g_last))
    S, o = lax.scan(step, S0, xs)
    o = jnp.transpose(o, (1, 0, 3, 2, 4)).reshape(B, L, H, dv)
    return o, S


def sliding_window_attention(q, k, v, k_prev, v_prev, sinks, start):
    B, L = q.shape[0], q.shape[1]
    qb_len = Q_BLOCK if L % Q_BLOCK == 0 else L
    n_blk = L // qb_len
    kb_len = WINDOW + qb_len
    k_all = jnp.concatenate([k_prev.astype(k.dtype), k], axis=1)
    v_all = jnp.concatenate([v_prev.astype(v.dtype), v], axis=1)
    kidx = jnp.arange(n_blk)[:, None] * qb_len + jnp.arange(kb_len)[None, :]
    k_blk = k_all[:, kidx]
    v_blk = v_all[:, kidx]
    q_blk = q.reshape(B, n_blk, qb_len, SWA_KV_HEADS, SWA_GROUP, SWA_HEAD_DIM)
    s = jnp.einsum('bnqhgd,bnkhd->bnhgqk', q_blk, k_blk).astype(jnp.float32) * SWA_HEAD_DIM ** -0.5
    dist = jnp.arange(qb_len)[:, None] + WINDOW - jnp.arange(kb_len)[None, :]
    key_pos = start - WINDOW + kidx
    valid = ((dist >= 0) & (dist <= WINDOW))[None] & (key_pos >= 0)[:, None, :]
    s = jnp.where(valid[None, :, None, None], s, -jnp.inf)
    sink = jnp.broadcast_to(sinks.astype(jnp.float32).reshape(SWA_KV_HEADS, SWA_GROUP, 1, 1), s.shape[:-1] + (1,))
    p = jax.nn.softmax(jnp.concatenate([s, sink], axis=-1), axis=-1)[..., :-1]
    o = jnp.einsum('bnhgqk,bnkhd->bnqhgd', p.astype(v.dtype), v_blk)
    return o.reshape(B, L, SWA_Q_W), k_all[:, -WINDOW:], v_all[:, -WINDOW:]


def swiglu(h, w_gate, w_up, w_down):
    return (jax.nn.silu(h @ w_gate) * (h @ w_up)) @ w_down


def moe_ffn(h, w_router, w_gate, w_up, w_down):
    logits = (h @ w_router).astype(jnp.float32)
    top_v, top_i = lax.top_k(logits, TOP_K)
    top_w = jax.nn.softmax(top_v, axis=-1)
    combine = jnp.sum(jax.nn.one_hot(top_i, N_EXPERTS, dtype=jnp.float32) * top_w[..., None], axis=-2).astype(h.dtype)
    out = combine[..., 0:1] * swiglu(h, w_gate[0], w_up[0], w_down[0])
    for e in range(1, N_EXPERTS):
        out = out + combine[..., e:e + 1] * swiglu(h, w_gate[e], w_up[e], w_down[e])
    return out


def trunk(x, c, conv_prev, gdn_prev, k_prev, v_prev, start,
          w_in, conv_w, a_log, dt_bias, gdn_norm, attn_sinks, w_branch, w_out,
          w_ada, b_ada, norm_mix, norm_ffn, w_gate_dense, w_up_dense, w_down_dense,
          w_router, w_gate_moe, w_up_moe, w_down_moe, final_norm):
    B, L, _ = x.shape
    f32 = jnp.float32
    sizes = [CONV_DIM, GDN_V_W, GDN_HEADS, GDN_HEADS, SWA_Q_W, SWA_KV_W, SWA_KV_W, D_MODEL, D_MODEL]
    cuts = [int(s) for s in np.cumsum(sizes)[:-1]]
    new_gdn, new_conv, new_k, new_v = [], [], [], []
    for l in range(DEPTH):
        ada = jax.nn.silu(c) @ w_ada[l] + b_ada[l]
        sh1, sc1, gt1, sh2, sc2, gt2 = jnp.split(ada[:, None, :], 6, axis=-1)
        h = rms_norm(x, norm_mix[l]) * (1 + sc1) + sh1
        qkv_a, z_a, a_a, b_a, q_b, k_b, v_b, gate_a, gate_b = jnp.split(h @ w_in[l], cuts, axis=-1)
        qkv_a, conv_buf = causal_conv(qkv_a, conv_prev[l], conv_w[l])
        q_a, k_a, v_a = jnp.split(qkv_a, [GDN_QK_W, 2 * GDN_QK_W], axis=-1)
        q_a = l2_norm(q_a.reshape(B, L, GDN_HEADS, GDN_DK)) * GDN_DK ** -0.5
        k_a = l2_norm(k_a.reshape(B, L, GDN_HEADS, GDN_DK))
        v_a = v_a.reshape(B, L, GDN_HEADS, GDN_DV).astype(f32)
        g_a = -jnp.exp(a_log[l].astype(f32)) * jax.nn.softplus(a_a.astype(f32) + dt_bias[l].astype(f32))
        beta_a = jax.nn.sigmoid(b_a.astype(f32))
        o_a, S = gated_delta_rule(q_a, k_a, v_a, g_a, beta_a, gdn_prev[l].astype(f32))
        o_a = rms_norm(o_a, gdn_norm[l]) * jax.nn.silu(z_a.reshape(B, L, GDN_HEADS, GDN_DV).astype(f32))
        o_a = o_a.reshape(B, L, GDN_V_W).astype(x.dtype)
        o_b, k_buf, v_buf = sliding_window_attention(
            q_b.reshape(B, L, SWA_HEADS, SWA_HEAD_DIM),
            k_b.reshape(B, L, SWA_KV_HEADS, SWA_HEAD_DIM),
            v_b.reshape(B, L, SWA_KV_HEADS, SWA_HEAD_DIM),
            k_prev[l], v_prev[l], attn_sinks[l], start)
        wb = w_branch[l]
        mixed = jax.nn.sigmoid(gate_a) * (o_a @ wb[:GDN_V_W]) + jax.nn.sigmoid(gate_b) * (o_b @ wb[GDN_V_W:])
        x = x + gt1 * (mixed @ w_out[l])
        h = rms_norm(x, norm_ffn[l]) * (1 + sc2) + sh2
        i = l // 2
        if l % 2 == 0:
            f = swiglu(h, w_gate_dense[i], w_up_dense[i], w_down_dense[i])
        else:
            f = moe_ffn(h, w_router[i], w_gate_moe[i], w_up_moe[i], w_down_moe[i])
        x = x + gt2 * f
        new_gdn.append(S)
        new_conv.append(conv_buf)
        new_k.append(k_buf)
        new_v.append(v_buf)
    y = rms_norm(x, final_norm)
    return y, jnp.stack(new_gdn), jnp.stack(new_conv), jnp.stack(new_k), jnp.stack(new_v)


def setup_inputs(seed: int = 0) -> dict:
    key = jax.random.key(seed)
    ks = iter(jax.random.split(key, 40))
    f32 = jnp.float32

    def nrm(shape, scale):
        return jax.random.normal(next(ks), shape, f32) * scale

    x_prompt = nrm((BATCH, SEQ, D_MODEL), 1.0)
    x_sample = nrm((DEC_BATCH, DEC_SEQ, D_MODEL), 1.0)
    c_prompt = nrm((BATCH, D_MODEL), 1.0)
    c_sample = nrm((DEC_BATCH, D_MODEL), 1.0)
    state_gdn = nrm((DEPTH, DEC_BATCH, GDN_HEADS, GDN_DK, GDN_DV), 0.5)
    state_conv = nrm((DEPTH, DEC_BATCH, CONV_W - 1, CONV_DIM), 1.0)
    cache_win_k = nrm((DEPTH, DEC_BATCH, WINDOW, SWA_KV_HEADS, SWA_HEAD_DIM), 1.0)
    cache_win_v = nrm((DEPTH, DEC_BATCH, WINDOW, SWA_KV_HEADS, SWA_HEAD_DIM), 1.0)
    w_in = nrm((DEPTH, D_MODEL, IN_DIM), D_MODEL ** -0.5)
    conv_w = nrm((DEPTH, CONV_W, CONV_DIM), CONV_W ** -0.5)
    a_log = jnp.log(jax.random.uniform(next(ks), (DEPTH, GDN_HEADS), f32, 1.0, 16.0))
    dt = jnp.exp(jax.random.uniform(next(ks), (DEPTH, GDN_HEADS), f32, math.log(1e-3), math.log(1e-1)))
    dt_bias = dt + jnp.log(-jnp.expm1(-dt))
    gdn_norm = 1.0 + nrm((DEPTH, GDN_DV), 0.02)
    attn_sinks = nrm((DEPTH, SWA_HEADS), 0.5)
    w_branch = nrm((DEPTH, MIX_WIDTH, D_MODEL), GDN_V_W ** -0.5)
    w_out = nrm((DEPTH, D_MODEL, D_MODEL), D_MODEL ** -0.5)
    w_ada = nrm((DEPTH, D_MODEL, 6 * D_MODEL), 0.5 * D_MODEL ** -0.5)
    b_ada = nrm((DEPTH, 6 * D_MODEL), 0.02)
    norm_mix = 1.0 + nrm((DEPTH, D_MODEL), 0.02)
    norm_ffn = 1.0 + nrm((DEPTH, D_MODEL), 0.02)
    w_gate_dense = nrm((N_DENSE, D_MODEL, D_FF), D_MODEL ** -0.5)
    w_up_dense = nrm((N_DENSE, D_MODEL, D_FF), D_MODEL ** -0.5)
    w_down_dense = nrm((N_DENSE, D_FF, D_MODEL), D_FF ** -0.5)
    w_router = nrm((N_MOE, D_MODEL, N_EXPERTS), D_MODEL ** -0.5)
    w_gate_moe = nrm((N_MOE, N_EXPERTS, D_MODEL, D_FF_EXPERT), D_MODEL ** -0.5)
    w_up_moe = nrm((N_MOE, N_EXPERTS, D_MODEL, D_FF_EXPERT), D_MODEL ** -0.5)
    w_down_moe = nrm((N_MOE, N_EXPERTS, D_FF_EXPERT, D_MODEL), D_FF_EXPERT ** -0.5)
    final_norm = 1.0 + nrm((D_MODEL,), 0.02)
    return {'x_prompt': x_prompt, 'x_sample': x_sample, 'c_prompt': c_prompt, 'c_sample': c_sample,
            'state_gdn': state_gdn, 'state_conv': state_conv, 'cache_win_k': cache_win_k, 'cache_win_v': cache_win_v,
            'w_in': w_in, 'conv_w': conv_w, 'a_log': a_log, 'dt_bias': dt_bias, 'gdn_norm': gdn_norm,
            'attn_sinks': attn_sinks, 'w_branch': w_branch, 'w_out': w_out, 'w_ada': w_ada, 'b_ada': b_ada,
            'norm_mix': norm_mix, 'norm_ffn': norm_ffn, 'w_gate_dense': w_gate_dense, 'w_up_dense': w_up_dense,
            'w_down_dense': w_down_dense, 'w_router': w_router, 'w_gate_moe': w_gate_moe, 'w_up_moe': w_up_moe,
            'w_down_moe': w_down_moe, 'final_norm': final_norm}


def reference(x_prompt, x_sample, c_prompt, c_sample, state_gdn, state_conv, cache_win_k, cache_win_v,
              w_in, conv_w, a_log, dt_bias, gdn_norm, attn_sinks, w_branch, w_out, w_ada, b_ada,
              norm_mix, norm_ffn, w_gate_dense, w_up_dense, w_down_dense, w_router, w_gate_moe,
              w_up_moe, w_down_moe, final_norm):
    weights = (w_in, conv_w, a_log, dt_bias, gdn_norm, attn_sinks, w_branch, w_out, w_ada, b_ada,
               norm_mix, norm_ffn, w_gate_dense, w_up_dense, w_down_dense, w_router, w_gate_moe,
               w_up_moe, w_down_moe, final_norm)
    dt = x_prompt.dtype
    conv0 = jnp.zeros((DEPTH, BATCH, CONV_W - 1, CONV_DIM), dt)
    gdn0 = jnp.zeros((DEPTH, BATCH, GDN_HEADS, GDN_DK, GDN_DV), jnp.float32)
    win0 = jnp.zeros((DEPTH, BATCH, WINDOW, SWA_KV_HEADS, SWA_HEAD_DIM), dt)
    y_prompt, gdn_p, conv_p, k_p, v_p = trunk(x_prompt, c_prompt, conv0, gdn0, win0, win0, 0, *weights)
    y_sample, gdn_s, conv_s, k_s, v_s = trunk(x_sample, c_sample, state_conv, state_gdn, cache_win_k, cache_win_v,
                                              PAST_LEN, *weights)
    return (y_prompt, y_sample,
            gdn_p.astype(state_gdn.dtype), conv_p.astype(state_conv.dtype),
            k_p.astype(cache_win_k.dtype), v_p.astype(cache_win_v.dtype),
            gdn_s.astype(state_gdn.dtype), conv_s.astype(state_conv.dtype),
            k_s.astype(cache_win_k.dtype), v_s.astype(cache_win_v.dtype))
```

```python
import functools

import jax
import jax.numpy as jnp
from jax import lax
from jax.experimental import pallas as pl
from jax.experimental.pallas import tpu as pltpu

F32 = jnp.float32
BF16 = jnp.bfloat16

D_MODEL = 1024
DEPTH = 2
GDN_HEADS = 4
GDN_DK = 128
GDN_DV = 128
CONV_W = 4
SWA_HEADS = 8
SWA_KV_HEADS = 2
SWA_HEAD_DIM = 64
SWA_GROUP = SWA_HEADS // SWA_KV_HEADS
WINDOW = 128
GDN_QK_W = GDN_HEADS * GDN_DK
GDN_V_W = GDN_HEADS * GDN_DV
CONV_DIM = 2 * GDN_QK_W + GDN_V_W
SWA_Q_W = SWA_HEADS * SWA_HEAD_DIM
SWA_KV_W = SWA_KV_HEADS * SWA_HEAD_DIM
N_EXPERTS = 8
TOP_K = 2
EPS = 1e-6

LANES = 128
SUBLANES = 8
VMEM_LIMIT = 56 * 1024 * 1024
NEG_BIG = -1e30

GDN_CHUNK = 64
TOKEN_TILE = 256
MOE_TILE = 1024
MOE_FF_TILE = 512
FFN_FF_TILE = 1408

IN_SPLITS = (CONV_DIM, GDN_V_W, SWA_Q_W, 2 * SWA_KV_W, 2 * D_MODEL, LANES)
IN_PAD_W = sum(IN_SPLITS)


def _silu(x):
    return x * jax.nn.sigmoid(x)


def _softplus(x):
    return jnp.maximum(x, 0.0) + jnp.log1p(jnp.exp(-jnp.abs(x)))


def _dot(a, b):
    return jnp.dot(a, b, preferred_element_type=F32)


def _dot_nt(a, b):
    return lax.dot_general(a, b, (((1,), (1,)), ((), ())), preferred_element_type=F32)


def _dot_tn(a, b):
    return lax.dot_general(a, b, (((0,), (0,)), ((), ())), preferred_element_type=F32)


def _split2(a):
    hi = a.astype(BF16)
    lo = (a - hi.astype(F32)).astype(BF16)
    return hi, lo


def _split3(a):
    hi = a.astype(BF16)
    r = a - hi.astype(F32)
    mid = r.astype(BF16)
    lo = (r - mid.astype(F32)).astype(BF16)
    return hi, mid, lo


def _dot3(a, b):
    ah, al = _split2(a)
    bh, bl = _split2(b)
    return _dot(ah, bh) + (_dot(ah, bl) + _dot(al, bh))


def _dot_exact_lhs(a01, b):
    a = a01.astype(BF16)
    bh, bm, bl = _split3(b)
    return _dot(a, bh) + (_dot(a, bm) + _dot(a, bl))


def _rms(x):
    return x * lax.rsqrt(jnp.mean(x * x, axis=-1, keepdims=True) + EPS)


def _params(sem):
    return pltpu.CompilerParams(dimension_semantics=sem, vmem_limit_bytes=VMEM_LIMIT)


def _ada_kernel(c_ref, w_ref, b_ref, o_ref):
    o_ref[...] = _dot3(_silu(c_ref[...]), w_ref[...]) + b_ref[...]


def _ada(c_all, w_ada, b_ada):
    m = c_all.shape[0]
    tn = 1536
    n = 6 * D_MODEL
    return pl.pallas_call(
        _ada_kernel,
        grid=(DEPTH, n // tn),
        in_specs=[
            pl.BlockSpec((m, D_MODEL), lambda l, j: (0, 0)),
            pl.BlockSpec((None, D_MODEL, tn), lambda l, j: (l, 0, j)),
            pl.BlockSpec((None, 1, tn), lambda l, j: (l, 0, j)),
        ],
        out_specs=pl.BlockSpec((None, m, tn), lambda l, j: (l, 0, j)),
        out_shape=jax.ShapeDtypeStruct((DEPTH, m, n), F32),
        compiler_params=_params(("arbitrary", "arbitrary")),
        name="ada",
    )(c_all, w_ada, b_ada.reshape(DEPTH, 1, n))


def _mod_spec(mod, chunk, tiles_per_mod):
    r = mod.shape[1]
    return pl.BlockSpec((None, r, D_MODEL), lambda i: (i // tiles_per_mod, 0, chunk))


def _in_kernel(x_ref, sh_ref, sc_ref, g_ref, w_ref, *out_refs):
    h = _rms(x_ref[...]) * g_ref[...]
    h = (h * (1.0 + sc_ref[...]) + sh_ref[...]).astype(BF16)
    off = 0
    for ref in out_refs:
        n = ref.shape[-1]
        ref[...] = _dot(h, w_ref[:, off:off + n])
        off += n


def _in_proj(x, mod, tiles_per_mod, norm_g, w_p):
    t = x.shape[0]
    tm = TOKEN_TILE
    return pl.pallas_call(
        _in_kernel,
        grid=(t // tm,),
        in_specs=[
            pl.BlockSpec((tm, D_MODEL), lambda i: (i, 0)),
            _mod_spec(mod, 0, tiles_per_mod),
            _mod_spec(mod, 1, tiles_per_mod),
            pl.BlockSpec((1, D_MODEL), lambda i: (0, 0)),
            pl.BlockSpec((D_MODEL, IN_PAD_W), lambda i: (0, 0)),
        ],
        out_specs=[pl.BlockSpec((tm, n), lambda i: (i, 0)) for n in IN_SPLITS],
        out_shape=[jax.ShapeDtypeStruct((t, n), F32) for n in IN_SPLITS],
        compiler_params=_params(("arbitrary",)),
        name="in_proj",
    )(x, mod, mod, norm_g, w_p)


def _inv_unit_lower(lm, nfac):
    c = lm.shape[0]
    r = lax.broadcasted_iota(jnp.int32, (c, c), 0)
    q = lax.broadcasted_iota(jnp.int32, (c, c), 1)
    t = jnp.where(r == q, 1.0, 0.0) - lm
    p = lm
    for _ in range(nfac - 1):
        p = _dot3(p, p)
        t = t + _dot3(t, p)
    return t


def _gdn_kernel(qkv_ref, ab_ref, s0_ref, cw_ref, alog_ref, dtb_ref, o_ref, s_ref, xs_ref,
                *, blk, lv, nfac):
    c = GDN_CHUNK
    nb = c // blk
    n = pl.program_id(1)
    hist = SUBLANES

    @pl.when(n == 0)
    def _():
        xs_ref[0:hist, :] = jnp.zeros((hist, CONV_DIM), F32)
        s_ref[...] = s0_ref[...]

    @pl.when(n > 0)
    def _():
        xs_ref[0:hist, :] = xs_ref[c:c + hist, :]

    xs_ref[hist:hist + c, :] = qkv_ref[...]

    def conv(c0):
        y = xs_ref[hist - 3:hist - 3 + c, c0:c0 + LANES] * cw_ref[0:1, c0:c0 + LANES]
        for i in range(1, CONV_W):
            y = y + xs_ref[hist - 3 + i:hist - 3 + i + c, c0:c0 + LANES] * cw_ref[i:i + 1, c0:c0 + LANES]
        return _silu(y)

    row = lax.broadcasted_iota(jnp.int32, (c, c), 0)
    col = lax.broadcasted_iota(jnp.int32, (c, c), 1)
    sh = blk.bit_length() - 1
    same = (row >> sh) == (col >> sh)
    incl = same & (row >= col)
    strict = same & (row > col)

    ab = ab_ref[...]
    g_all = -jnp.exp(alog_ref[...]) * _softplus(ab + dtb_ref[...])
    beta_all = jax.nn.sigmoid(ab)
    if lv < blk:
        rvalid = (lax.broadcasted_iota(jnp.int32, (c, 1), 0) & (blk - 1)) >= (blk - lv)
        g_all = jnp.where(rvalid, g_all, 0.0)
        beta_all = jnp.where(rvalid, beta_all, 0.0)
    gcum = _dot_exact_lhs(jnp.where(incl, 1.0, 0.0), g_all)
    gtot = _dot_exact_lhs(jnp.where(same, 1.0, 0.0), g_all)
    gcum_t = jnp.concatenate([gcum, jnp.zeros((LANES - c, LANES), F32)], axis=0).T

    for h in range(GDN_HEADS):
        q = conv(h * GDN_DK)
        k = conv(GDN_QK_W + h * GDN_DK)
        v = conv(2 * GDN_QK_W + h * GDN_DV)
        if lv < blk:
            q = jnp.where(rvalid, q, 0.0)
            k = jnp.where(rvalid, k, 0.0)
            v = jnp.where(rvalid, v, 0.0)
        q = q * lax.rsqrt(jnp.sum(q * q, axis=-1, keepdims=True) + EPS) * (GDN_DK ** -0.5)
        k = k * lax.rsqrt(jnp.sum(k * k, axis=-1, keepdims=True) + EPS)
        beta = beta_all[:, GDN_HEADS + h:GDN_HEADS + h + 1]
        gc = gcum[:, h:h + 1]
        gr = gcum_t[h:h + 1, 0:c]
        gl = gtot[:, h:h + 1]
        decay = jnp.where(incl, jnp.exp(jnp.minimum(gc - gr, 0.0)), 0.0)
        kb = k * beta
        k16 = k.astype(BF16)
        lm = jnp.where(strict, _dot_nt(kb.astype(BF16), k16) * decay, 0.0)
        tinv = _inv_unit_lower(lm, nfac)
        eg = jnp.exp(gc)
        rhs = jnp.concatenate([v * beta, kb * eg], axis=1)
        rh, rl = _split2(rhs)
        t16 = tinv.astype(BF16)
        x = _dot(t16, rh) + _dot(t16, rl)
        u = x[:, :GDN_DV]
        w = x[:, GDN_DV:]
        qk = (_dot_nt(q.astype(BF16), k16) * decay).astype(BF16)
        qd = q * eg
        kd = k * jnp.exp(gl - gc)
        outs = []
        for j in range(nb):
            r0 = j * blk
            s = s_ref[j, h]
            wq = jnp.concatenate([w[r0:r0 + blk], qd[r0:r0 + blk]], axis=0).astype(BF16)
            rr = _dot(wq, s.astype(BF16))
            vnew = u[r0:r0 + blk] - rr[:blk]
            outs.append((rr[blk:], vnew))
        vnew = outs[0][1] if nb == 1 else jnp.concatenate([o[1] for o in outs], axis=0)
        qs = outs[0][0] if nb == 1 else jnp.concatenate([o[0] for o in outs], axis=0)
        v16 = vnew.astype(BF16)
        o_ref[:, h * GDN_DV:(h + 1) * GDN_DV] = qs + _dot(qk, v16)
        rblk = lax.broadcasted_iota(jnp.int32, (c, 1), 0) >> sh
        for j in range(nb):
            kdj = kd if nb == 1 else jnp.where(rblk == j, kd, 0.0)
            egl = jnp.exp(gl[j * blk:j * blk + 1, :])
            s_ref[j, h] = egl * s_ref[j, h] + _dot_tn(kdj.astype(BF16), v16)


def _gdn(qkv, ab, s0, conv_w, alog_row, dtb_row, *, blk, lv, chunks_per_seq):
    rows = qkv.shape[0]
    c = GDN_CHUNK
    nb = c // blk
    ns = s0.shape[0]
    nfac = max(1, (lv - 1).bit_length())
    nseq = rows // (c * chunks_per_seq)
    assert nseq * nb == ns
    kern = functools.partial(_gdn_kernel, blk=blk, lv=lv, nfac=nfac)
    return pl.pallas_call(
        kern,
        grid=(nseq, chunks_per_seq),
        in_specs=[
            pl.BlockSpec((c, CONV_DIM), lambda i, n: (i * chunks_per_seq + n, 0)),
            pl.BlockSpec((c, LANES), lambda i, n: (i * chunks_per_seq + n, 0)),
            pl.BlockSpec((nb, GDN_HEADS, GDN_DK, GDN_DV), lambda i, n: (i, 0, 0, 0)),
            pl.BlockSpec((CONV_W, CONV_DIM), lambda i, n: (0, 0)),
            pl.BlockSpec((1, LANES), lambda i, n: (0, 0)),
            pl.BlockSpec((1, LANES), lambda i, n: (0, 0)),
        ],
        out_specs=[
            pl.BlockSpec((c, GDN_V_W), lambda i, n: (i * chunks_per_seq + n, 0)),
            pl.BlockSpec((nb, GDN_HEADS, GDN_DK, GDN_DV), lambda i, n: (i, 0, 0, 0)),
        ],
        out_shape=[
            jax.ShapeDtypeStruct((rows, GDN_V_W), F32),
            jax.ShapeDtypeStruct(s0.shape, F32),
        ],
        scratch_shapes=[pltpu.VMEM((SUBLANES + c, CONV_DIM), F32)],
        compiler_params=_params(("arbitrary", "arbitrary")),
        name="gdn",
    )(qkv, ab, s0, conv_w, alog_row, dtb_row)


def _swa_kernel(q_ref, kc_ref, vc_ref, kp_ref, vp_ref, sink_ref, o_ref, *scratch,
                qb, bb, prev_is_cache):
    w = WINDOW
    n = pl.program_id(1)
    if qb < w:
        kpad_ref, vpad_ref = scratch
        kpad_ref[...] = jnp.zeros((w, SWA_KV_W), F32)
        vpad_ref[...] = jnp.zeros((w, SWA_KV_W), F32)
    qi = lax.broadcasted_iota(jnp.int32, (qb, w), 0)
    kj = lax.broadcasted_iota(jnp.int32, (qb, w), 1)
    prev_ok = kj >= qi
    if not prev_is_cache:
        prev_ok = prev_ok & (n > 0)
    cur_ok = kj <= qi
    scale = SWA_HEAD_DIM ** -0.5
    for b in range(bb):
        r0 = b * qb
        if qb < w:
            kpad_ref[0:qb, :] = kc_ref[r0:r0 + qb, :]
            vpad_ref[0:qb, :] = vc_ref[r0:r0 + qb, :]
            kc_all, vc_all = kpad_ref[...], vpad_ref[...]
        else:
            kc_all, vc_all = kc_ref[...], vc_ref[...]
        kp_all = kp_ref[b] if prev_is_cache else kp_ref[...]
        vp_all = vp_ref[b] if prev_is_cache else vp_ref[...]
        for hk in range(SWA_KV_HEADS):
            c0 = hk * SWA_HEAD_DIM
            kc = kc_all[:, c0:c0 + SWA_HEAD_DIM].astype(BF16)
            vc = vc_all[:, c0:c0 + SWA_HEAD_DIM].astype(BF16)
            kp = kp_all[:, c0:c0 + SWA_HEAD_DIM].astype(BF16)
            vp = vp_all[:, c0:c0 + SWA_HEAD_DIM].astype(BF16)
            for g in range(SWA_GROUP):
                hd = hk * SWA_GROUP + g
                q0 = hd * SWA_HEAD_DIM
                q = q_ref[r0:r0 + qb, q0:q0 + SWA_HEAD_DIM].astype(BF16)
                s1 = jnp.where(prev_ok, _dot_nt(q, kp) * scale, NEG_BIG)
                s2 = jnp.where(cur_ok, _dot_nt(q, kc) * scale, NEG_BIG)
                sink = sink_ref[0:1, hd:hd + 1]
                m = jnp.maximum(jnp.maximum(jnp.max(s1, axis=-1, keepdims=True),
                                            jnp.max(s2, axis=-1, keepdims=True)), sink)
                p1 = jnp.exp(s1 - m)
                p2 = jnp.exp(s2 - m)
                den = (jnp.sum(p1, axis=-1, keepdims=True) + jnp.sum(p2, axis=-1, keepdims=True)
                       + jnp.exp(sink - m))
                acc = _dot(p1.astype(BF16), vp) + _dot(p2.astype(BF16), vc)
                o_ref[r0:r0 + qb, q0:q0 + SWA_HEAD_DIM] = acc / den


def _swa(q, kv, sink_row, *, qb, bb, blocks_per_seq, cache_k=None, cache_v=None):
    rows = q.shape[0]
    w = WINDOW
    prev_is_cache = cache_k is not None
    steps = rows // (qb * bb)
    nseq = steps // blocks_per_seq
    kern = functools.partial(_swa_kernel, qb=qb, bb=bb, prev_is_cache=prev_is_cache)
    cur = lambda i, n: (i * blocks_per_seq + n, 0)
    if prev_is_cache:
        kp_spec = pl.BlockSpec((bb, w, SWA_KV_W), lambda i, n: (i, 0, 0))
        vp_spec = kp_spec
        kp_arr, vp_arr = cache_k, cache_v
    else:
        assert qb == w and bb == 1
        kp_spec = pl.BlockSpec((w, SWA_KV_W), lambda i, n: (i * blocks_per_seq + jnp.maximum(n - 1, 0), 0))
        vp_spec = pl.BlockSpec((w, SWA_KV_W), lambda i, n: (i * blocks_per_seq + jnp.maximum(n - 1, 0), 1))
        kp_arr, vp_arr = kv, kv
    scratch = [] if qb == w else [pltpu.VMEM((w, SWA_KV_W), F32), pltpu.VMEM((w, SWA_KV_W), F32)]
    return pl.pallas_call(
        kern,
        grid=(nseq, blocks_per_seq),
        in_specs=[
            pl.BlockSpec((qb * bb, SWA_Q_W), cur),
            pl.BlockSpec((qb * bb, SWA_KV_W), cur),
            pl.BlockSpec((qb * bb, SWA_KV_W), lambda i, n: (i * blocks_per_seq + n, 1)),
            kp_spec,
            vp_spec,
            pl.BlockSpec((1, LANES), lambda i, n: (0, 0)),
        ],
        out_specs=pl.BlockSpec((qb * bb, SWA_Q_W), cur),
        out_shape=jax.ShapeDtypeStruct((rows, SWA_Q_W), F32),
        scratch_shapes=scratch,
        compiler_params=_params(("arbitrary", "arbitrary")),
        name="swa",
    )(q, kv, kv, kp_arr, vp_arr, sink_row)


def _mix_kernel(o_ref, z_ref, ob_ref, gates_ref, x_ref, gt_ref, gn_ref, wb_ref, wo_ref, y_ref):
    parts = []
    for h in range(GDN_HEADS):
        sl = slice(h * GDN_DV, (h + 1) * GDN_DV)
        parts.append(_rms(o_ref[:, sl]) * gn_ref[...] * _silu(z_ref[:, sl]))
    oa = jnp.concatenate(parts, axis=1).astype(BF16)
    pa = _dot(oa, wb_ref[0:GDN_V_W, :])
    pb = _dot(ob_ref[...].astype(BF16), wb_ref[GDN_V_W:, :])
    mixed = (jax.nn.sigmoid(gates_ref[:, 0:D_MODEL]) * pa
             + jax.nn.sigmoid(gates_ref[:, D_MODEL:]) * pb)
    y_ref[...] = x_ref[...] + gt_ref[...] * _dot(mixed.astype(BF16), wo_ref[...])


def _mix(o_raw, z, ob, gates, x, mod, tiles_per_mod, gn_row, wb, wo):
    t = x.shape[0]
    tm = TOKEN_TILE
    row = lambda n: pl.BlockSpec((tm, n), lambda i: (i, 0))
    full = lambda a: pl.BlockSpec(a.shape, lambda i: (0, 0))
    return pl.pallas_call(
        _mix_kernel,
        grid=(t // tm,),
        in_specs=[row(GDN_V_W), row(GDN_V_W), row(SWA_Q_W), row(2 * D_MODEL), row(D_MODEL),
                  _mod_spec(mod, 2, tiles_per_mod), full(gn_row), full(wb), full(wo)],
        out_specs=row(D_MODEL),
        out_shape=jax.ShapeDtypeStruct((t, D_MODEL), F32),
        compiler_params=_params(("arbitrary",)),
        name="mix",
    )(o_raw, z, ob, gates, x, mod, gn_row, wb, wo)


def _ffn_kernel(x_ref, sh_ref, sc_ref, gt_ref, g_ref, wg_ref, wu_ref, wd_ref, y_ref, h_ref, acc_ref):
    j = pl.program_id(1)

    @pl.when(j == 0)
    def _():
        h = _rms(x_ref[...]) * g_ref[...]
        h_ref[...] = (h * (1.0 + sc_ref[...]) + sh_ref[...]).astype(BF16)

    h = h_ref[...]
    a = (_silu(_dot(h, wg_ref[...])) * _dot(h, wu_ref[...])).astype(BF16)
    d = _dot(a, wd_ref[...])

    @pl.when(j == 0)
    def _():
        acc_ref[...] = d

    @pl.when(j > 0)
    def _():
        acc_ref[...] += d

    @pl.when(j == pl.num_programs(1) - 1)
    def _():
        y_ref[...] = x_ref[...] + gt_ref[...] * acc_ref[...]


def _ffn(x, mod, tiles_per_mod, norm_g, wg, wu, wd):
    t = x.shape[0]
    tm = 2 * TOKEN_TILE
    tpm = max(1, tiles_per_mod // 2)
    d_ff = wg.shape[1]
    tf = FFN_FF_TILE
    r = mod.shape[1]
    mspec = lambda chunk: pl.BlockSpec((None, r, D_MODEL), lambda i, j: (i // tpm, 0, chunk))
    return pl.pallas_call(
        _ffn_kernel,
        grid=(t // tm, d_ff // tf),
        in_specs=[
            pl.BlockSpec((tm, D_MODEL), lambda i, j: (i, 0)),
            mspec(3), mspec(4), mspec(5),
            pl.BlockSpec((1, D_MODEL), lambda i, j: (0, 0)),
            pl.BlockSpec((D_MODEL, tf), lambda i, j: (0, j)),
            pl.BlockSpec((D_MODEL, tf), lambda i, j: (0, j)),
            pl.BlockSpec((tf, D_MODEL), lambda i, j: (j, 0)),
        ],
        out_specs=pl.BlockSpec((tm, D_MODEL), lambda i, j: (i, 0)),
        out_shape=jax.ShapeDtypeStruct((t, D_MODEL), F32),
        scratch_shapes=[pltpu.VMEM((tm, D_MODEL), BF16), pltpu.VMEM((tm, D_MODEL), F32)],
        compiler_params=_params(("arbitrary", "arbitrary")),
        name="ffn",
    )(x, mod, mod, mod, norm_g, wg, wu, wd)


def _route_kernel(x_ref, sh_ref, sc_ref, g_ref, wr_ref, h_ref, idx_ref, wt_ref):
    h = _rms(x_ref[...]) * g_ref[...]
    h = h * (1.0 + sc_ref[...]) + sh_ref[...]
    h_ref[...] = h.astype(BF16)
    logits = _dot3(h, wr_ref[...])
    lane_i = lax.broadcasted_iota(jnp.int32, logits.shape, 1)
    lane = lane_i.astype(F32)
    logits = jnp.where(lane_i < N_EXPERTS, logits, NEG_BIG)
    m1 = jnp.max(logits, axis=-1, keepdims=True)
    i1 = jnp.min(jnp.where(logits == m1, lane, float(LANES)), axis=-1, keepdims=True)
    rest = jnp.where(lane == i1, NEG_BIG, logits)
    m2 = jnp.max(rest, axis=-1, keepdims=True)
    i2 = jnp.min(jnp.where(rest == m2, lane, float(LANES)), axis=-1, keepdims=True)
    e = jnp.exp(m2 - m1)
    den = 1.0 + e
    idx_ref[...] = jnp.where(lane_i == 0, i1, jnp.where(lane_i == 1, i2, 0.0)).astype(jnp.int32)
    wt_ref[...] = jnp.where(lane_i == 0, 1.0 / den, jnp.where(lane_i == 1, e / den, 0.0))


def _route(x, mod, tiles_per_mod, norm_g, wr_pad):
    t = x.shape[0]
    tm = TOKEN_TILE
    row = lambda n: pl.BlockSpec((tm, n), lambda i: (i, 0))
    return pl.pallas_call(
        _route_kernel,
        grid=(t // tm,),
        in_specs=[row(D_MODEL), _mod_spec(mod, 3, tiles_per_mod), _mod_spec(mod, 4, tiles_per_mod),
                  pl.BlockSpec((1, D_MODEL), lambda i: (0, 0)),
                  pl.BlockSpec((D_MODEL, LANES), lambda i: (0, 0))],
        out_specs=[row(D_MODEL), row(LANES), row(LANES)],
        out_shape=[jax.ShapeDtypeStruct((t, D_MODEL), BF16),
                   jax.ShapeDtypeStruct((t, LANES), jnp.int32),
                   jax.ShapeDtypeStruct((t, LANES), F32)],
        compiler_params=_params(("arbitrary",)),
        name="route",
    )(x, mod, mod, norm_g, wr_pad)


def _moe_kernel(te_ref, nu_ref, x_ref, rw_ref, wg_ref, wu_ref, wd_ref, o_ref):
    i = pl.program_id(0)
    j = pl.program_id(1)
    used = i < nu_ref[0]

    @pl.when(used)
    def _():
        x = x_ref[...]
        a = (_silu(_dot(x, wg_ref[...].astype(BF16))) * _dot(x, wu_ref[...].astype(BF16))).astype(BF16)
        d = _dot(a, wd_ref[...].astype(BF16))

        @pl.when(j == 0)
        def _():
            o_ref[...] = d

        @pl.when(j > 0)
        def _():
            o_ref[...] += d

        @pl.when(j == pl.num_programs(1) - 1)
        def _():
            o_ref[...] = o_ref[...] * rw_ref[...]

    @pl.when(jnp.logical_not(used) & (j == 0))
    def _():
        o_ref[...] = jnp.zeros_like(o_ref)


def _moe(xs, row_w, tile_expert, n_used, wg, wu, wd):
    p = xs.shape[0]
    tm = MOE_TILE
    tf = MOE_FF_TILE
    d_ff = wg.shape[2]
    nj = d_ff // tf

    def jj(i, j, nu):
        return jnp.where(i < nu[0], j, nj - 1)

    grid_spec = pltpu.PrefetchScalarGridSpec(
        num_scalar_prefetch=2,
        grid=(p // tm, nj),
        in_specs=[
            pl.BlockSpec((tm, D_MODEL), lambda i, j, te, nu: (i, 0)),
            pl.BlockSpec((tm, 1), lambda i, j, te, nu: (i, 0)),
            pl.BlockSpec((None, D_MODEL, tf), lambda i, j, te, nu: (te[i], 0, jj(i, j, nu))),
            pl.BlockSpec((None, D_MODEL, tf), lambda i, j, te, nu: (te[i], 0, jj(i, j, nu))),
            pl.BlockSpec((None, tf, D_MODEL), lambda i, j, te, nu: (te[i], jj(i, j, nu), 0)),
        ],
        out_specs=pl.BlockSpec((tm, D_MODEL), lambda i, j, te, nu: (i, 0)),
    )
    return pl.pallas_call(
        _moe_kernel,
        grid_spec=grid_spec,
        out_shape=jax.ShapeDtypeStruct((p, D_MODEL), F32),
        compiler_params=_params(("arbitrary", "arbitrary")),
        name="moe",
    )(tile_expert, n_used, xs, row_w, wg, wu, wd)


def _dispatch(top_i, top_w):
    t = top_i.shape[0]
    tm = MOE_TILE
    n_assign = t * TOP_K
    n_tiles = -(-n_assign // tm) + N_EXPERTS
    e_flat = top_i.reshape(-1)
    onehot = (e_flat[:, None] == jnp.arange(N_EXPERTS, dtype=jnp.int32)[None, :]).astype(jnp.int32)
    csum = jnp.cumsum(onehot, axis=0)
    counts = csum[-1]
    rank = jnp.take_along_axis(csum, e_flat[:, None], axis=1)[:, 0] - 1
    tiles_e = (counts + tm - 1) // tm
    tile_end = jnp.cumsum(tiles_e)
    tile_start = tile_end - tiles_e
    pos = tile_start[e_flat] * tm + rank
    n_used = tile_end[-1]
    tile_ids = jnp.arange(n_tiles, dtype=jnp.int32)
    tile_expert = jnp.sum((tile_ids[:, None] >= tile_end[None, :]).astype(jnp.int32), axis=1)
    last_e = jnp.max(jnp.where(tiles_e > 0, jnp.arange(N_EXPERTS, dtype=jnp.int32), 0))
    tile_expert = jnp.where(tile_ids < n_used, tile_expert, last_e).astype(jnp.int32)
    p = n_tiles * tm
    tok = jnp.arange(n_assign, dtype=jnp.int32) // TOP_K
    row_token = jnp.zeros((p,), jnp.int32).at[pos].set(tok)
    row_w = jnp.zeros((p,), F32).at[pos].set(top_w.reshape(-1))
    return pos.reshape(t, TOP_K), row_token, row_w[:, None], tile_expert, n_used.reshape(1).astype(jnp.int32)


def _combine_kernel(x_ref, ya_ref, yb_ref, gt_ref, g_ref, y_ref):
    x = x_ref[...] + gt_ref[...] * (ya_ref[...] + yb_ref[...])
    y_ref[...] = _rms(x) * g_ref[...]


def _combine(x, ya, yb, mod, tiles_per_mod, norm_g):
    t = x.shape[0]
    tm = TOKEN_TILE
    row = pl.BlockSpec((tm, D_MODEL), lambda i: (i, 0))
    return pl.pallas_call(
        _combine_kernel,
        grid=(t // tm,),
        in_specs=[row, row, row, _mod_spec(mod, 5, tiles_per_mod),
                  pl.BlockSpec((1, D_MODEL), lambda i: (0, 0))],
        out_specs=row,
        out_shape=jax.ShapeDtypeStruct((t, D_MODEL), F32),
        compiler_params=_params(("arbitrary",)),
        name="combine",
    )(x, ya, yb, mod, norm_g)


def _lane_row(vals):
    return jnp.zeros((1, LANES), F32).at[0, :vals.shape[0]].set(vals.astype(F32))


def _permute_w_in(w):
    c = [CONV_DIM, CONV_DIM + GDN_V_W]
    a0 = c[1]
    q0 = a0 + 2 * GDN_HEADS
    k0 = q0 + SWA_Q_W
    g0 = k0 + 2 * SWA_KV_W
    cols = [w[:, :c[0]], w[:, c[0]:c[1]], w[:, q0:k0], w[:, k0:g0], w[:, g0:],
            w[:, a0:q0], jnp.zeros((D_MODEL, LANES - 2 * GDN_HEADS), w.dtype)]
    return jnp.concatenate(cols, axis=1).astype(BF16)


def kernel(x_prompt, x_sample, c_prompt, c_sample, state_gdn, state_conv, cache_win_k, cache_win_v, w_in, conv_w, a_log, dt_bias, gdn_norm, attn_sinks, w_branch, w_out, w_ada, b_ada, norm_mix, norm_ffn, w_gate_dense, w_up_dense, w_down_dense, w_router, w_gate_moe, w_up_moe, w_down_moe, final_norm):
    bp, lp, _ = x_prompt.shape
    bs, ls, _ = x_sample.shape
    tp, ts = bp * lp, bs * ls
    tm = TOKEN_TILE
    pad_rows = SUBLANES - ls

    m_pad = -(-(bp + bs) // SUBLANES) * SUBLANES
    c_all = jnp.concatenate([c_prompt, c_sample, jnp.zeros((m_pad - bp - bs, D_MODEL), F32)], axis=0)
    ada = _ada(c_all, w_ada, b_ada)

    xp = x_prompt.reshape(tp, D_MODEL)
    xs = x_sample.reshape(ts, D_MODEL)
    outs = {k: [] for k in ("gdn_p", "conv_p", "k_p", "v_p", "gdn_s", "conv_s", "k_s", "v_s")}
    for l in range(DEPTH):
        mod_p = ada[l, :bp].reshape(bp, 1, 6 * D_MODEL)
        mod_s = jnp.repeat(ada[l, bp:bp + bs], ls, axis=0).reshape(ts // tm, tm, 6 * D_MODEL)
        tpm_p = lp // tm
        w_p = _permute_w_in(w_in[l])
        nmix = norm_mix[l].reshape(1, D_MODEL)
        nffn = norm_ffn[l].reshape(1, D_MODEL)
        alog_row, dtb_row = _lane_row(a_log[l]), _lane_row(dt_bias[l])
        sink_row = _lane_row(attn_sinks[l])
        gn_row = gdn_norm[l].reshape(1, GDN_DV)
        wb = w_branch[l].astype(BF16)
        wo = w_out[l].astype(BF16)

        qkv_p, z_p, qb_p, kv_p, gates_p, ab_p = _in_proj(xp, mod_p, tpm_p, nmix, w_p)
        s0_p = jnp.zeros((bp, GDN_HEADS, GDN_DK, GDN_DV), F32)
        o_p, s_p = _gdn(qkv_p, ab_p, s0_p, conv_w[l], alog_row, dtb_row,
                        blk=GDN_CHUNK, lv=GDN_CHUNK, chunks_per_seq=lp // GDN_CHUNK)
        ob_p = _swa(qb_p, kv_p, sink_row, qb=WINDOW, bb=1, blocks_per_seq=lp // WINDOW)
        xp = _mix(o_p, z_p, ob_p, gates_p, xp, mod_p, tpm_p, gn_row, wb, wo)
        outs["gdn_p"].append(s_p)
        outs["conv_p"].append(qkv_p.reshape(bp, lp, CONV_DIM)[:, lp - (CONV_W - 1):])
        kv3 = kv_p.reshape(bp, lp, 2 * SWA_KV_W)[:, lp - WINDOW:]
        outs["k_p"].append(kv3[..., :SWA_KV_W].reshape(bp, WINDOW, SWA_KV_HEADS, SWA_HEAD_DIM))
        outs["v_p"].append(kv3[..., SWA_KV_W:].reshape(bp, WINDOW, SWA_KV_HEADS, SWA_HEAD_DIM))

        qkv_s, z_s, qb_s, kv_s, gates_s, ab_s = _in_proj(xs, mod_s, 1, nmix, w_p)
        qkv_s3 = qkv_s.reshape(bs, ls, CONV_DIM)
        hist = jnp.concatenate([jnp.zeros((bs, SUBLANES - ls - (CONV_W - 1), CONV_DIM), F32),
                                state_conv[l], qkv_s3], axis=1)
        ab_blk = jnp.concatenate([jnp.zeros((bs, pad_rows, LANES), F32), ab_s.reshape(bs, ls, LANES)], axis=1)
        o_s, s_s = _gdn(hist.reshape(bs * SUBLANES, CONV_DIM), ab_blk.reshape(bs * SUBLANES, LANES),
                        state_gdn[l], conv_w[l], alog_row, dtb_row,
                        blk=SUBLANES, lv=ls, chunks_per_seq=1)
        o_s = o_s.reshape(bs, SUBLANES, GDN_V_W)[:, pad_rows:].reshape(ts, GDN_V_W)
        pad_after = lambda a: jnp.concatenate(
            [a.reshape(bs, ls, -1), jnp.zeros((bs, pad_rows, a.shape[-1]), F32)], axis=1).reshape(bs * SUBLANES, -1)
        ck = cache_win_k[l].reshape(bs, WINDOW, SWA_KV_W)
        cv = cache_win_v[l].reshape(bs, WINDOW, SWA_KV_W)
        ob_s = _swa(pad_after(qb_s), pad_after(kv_s), sink_row, qb=SUBLANES, bb=8, blocks_per_seq=1,
                    cache_k=ck, cache_v=cv)
        ob_s = ob_s.reshape(bs, SUBLANES, SWA_Q_W)[:, :ls].reshape(ts, SWA_Q_W)
        xs = _mix(o_s, z_s, ob_s, gates_s, xs, mod_s, 1, gn_row, wb, wo)
        outs["gdn_s"].append(s_s)
        outs["conv_s"].append(jnp.concatenate([state_conv[l], qkv_s3], axis=1)[:, -(CONV_W - 1):])
        kv_s3 = kv_s.reshape(bs, ls, 2 * SWA_KV_W)
        knew = jnp.concatenate([ck, kv_s3[..., :SWA_KV_W]], axis=1)[:, -WINDOW:]
        vnew = jnp.concatenate([cv, kv_s3[..., SWA_KV_W:]], axis=1)[:, -WINDOW:]
        outs["k_s"].append(knew.reshape(bs, WINDOW, SWA_KV_HEADS, SWA_HEAD_DIM))
        outs["v_s"].append(vnew.reshape(bs, WINDOW, SWA_KV_HEADS, SWA_HEAD_DIM))

        i = l // 2
        if l % 2 == 0:
            wg, wu, wd = (w_gate_dense[i].astype(BF16), w_up_dense[i].astype(BF16),
                          w_down_dense[i].astype(BF16))
            xp = _ffn(xp, mod_p, tpm_p, nffn, wg, wu, wd)
            xs = _ffn(xs, mod_s.reshape(ts // (2 * tm), 2 * tm, 6 * D_MODEL), 1, nffn, wg, wu, wd)
        else:
            wr_pad = jnp.concatenate([w_router[i], jnp.zeros((D_MODEL, LANES - N_EXPERTS), F32)], axis=1)
            h_p, idx_p, wt_p = _route(xp, mod_p, tpm_p, nffn, wr_pad)
            h_s, idx_s, wt_s = _route(xs, mod_s, 1, nffn, wr_pad)
            h_all = jnp.concatenate([h_p, h_s], axis=0)
            top_i = jnp.concatenate([idx_p[:, :TOP_K], idx_s[:, :TOP_K]], axis=0)
            top_w = jnp.concatenate([wt_p[:, :TOP_K], wt_s[:, :TOP_K]], axis=0)
            pos, row_token, row_w, tile_expert, n_used = _dispatch(top_i, top_w)
            y = _moe(jnp.take(h_all, row_token, axis=0), row_w, tile_expert, n_used,
                     w_gate_moe[i], w_up_moe[i], w_down_moe[i])
            ya = jnp.take(y, pos[:, 0], axis=0)
            yb = jnp.take(y, pos[:, 1], axis=0)
            fin = final_norm.reshape(1, D_MODEL)
            assert l == DEPTH - 1
            xp = _combine(xp, ya[:tp], yb[:tp], mod_p, tpm_p, fin)
            xs = _combine(xs, ya[tp:], yb[tp:], mod_s, 1, fin)

    st = lambda k: jnp.stack(outs[k])
    return (xp.reshape(bp, lp, D_MODEL), xs.reshape(bs, ls, D_MODEL),
            st("gdn_p"), st("conv_p"), st("k_p"), st("v_p"),
            st("gdn_s"), st("conv_s"), st("k_s"), st("v_s"))
```

```python
import functools

import jax
import jax.numpy as jnp
from jax import lax
from jax.experimental import pallas as pl
from jax.experimental.pallas import tpu as pltpu

F32 = jnp.float32
BF16 = jnp.bfloat16

D_MODEL = 1024
DEPTH = 2
GDN_HEADS = 4
GDN_DK = 128
GDN_DV = 128
CONV_W = 4
SWA_HEADS = 8
SWA_KV_HEADS = 2
SWA_HEAD_DIM = 64
SWA_GROUP = SWA_HEADS // SWA_KV_HEADS
WINDOW = 128
GDN_QK_W = GDN_HEADS * GDN_DK
GDN_V_W = GDN_HEADS * GDN_DV
CONV_DIM = 2 * GDN_QK_W + GDN_V_W
SWA_Q_W = SWA_HEADS * SWA_HEAD_DIM
SWA_KV_W = SWA_KV_HEADS * SWA_HEAD_DIM
N_EXPERTS = 8
TOP_K = 2
EPS = 1e-6

LANES = 128
SUBLANES = 8
VMEM_LIMIT = 56 * 1024 * 1024
NEG_BIG = -1e30

GDN_CHUNK = 64
TOKEN_TILE = 256
MOE_TILE = 1024
MOE_FF_TILE = 512
FFN_FF_TILE = 1408

IN_SPLITS = (CONV_DIM, GDN_V_W, SWA_Q_W, 2 * SWA_KV_W, 2 * D_MODEL, LANES)
IN_PAD_W = sum(IN_SPLITS)


def _silu(x):
    return x * jax.nn.sigmoid(x)


def _softplus(x):
    return jnp.maximum(x, 0.0) + jnp.log1p(jnp.exp(-jnp.abs(x)))


def _dot(a, b):
    return jnp.dot(a, b, preferred_element_type=F32)


def _dot_nt(a, b):
    return lax.dot_general(a, b, (((1,), (1,)), ((), ())), preferred_element_type=F32)


def _dot_tn(a, b):
    return lax.dot_general(a, b, (((0,), (0,)), ((), ())), preferred_element_type=F32)


def _split2(a):
    hi = a.astype(BF16)
    lo = (a - hi.astype(F32)).astype(BF16)
    return hi, lo


def _split3(a):
    hi = a.astype(BF16)
    r = a - hi.astype(F32)
    mid = r.astype(BF16)
    lo = (r - mid.astype(F32)).astype(BF16)
    return hi, mid, lo


def _dot3(a, b):
    ah, al = _split2(a)
    bh, bl = _split2(b)
    return _dot(ah, bh) + (_dot(ah, bl) + _dot(al, bh))


def _dot_exact_lhs(a01, b):
    a = a01.astype(BF16)
    bh, bm, bl = _split3(b)
    return _dot(a, bh) + (_dot(a, bm) + _dot(a, bl))


def _rms(x):
    return x * lax.rsqrt(jnp.mean(x * x, axis=-1, keepdims=True) + EPS)


def _params(sem):
    return pltpu.CompilerParams(dimension_semantics=sem, vmem_limit_bytes=VMEM_LIMIT)


def _ada_kernel(c_ref, w_ref, b_ref, o_ref):
    o_ref[...] = _dot3(_silu(c_ref[...]), w_ref[...]) + b_ref[...]


def _ada(c_all, w_ada, b_ada):
    m = c_all.shape[0]
    tn = 1536
    n = 6 * D_MODEL
    return pl.pallas_call(
        _ada_kernel,
        grid=(DEPTH, n // tn),
        in_specs=[
            pl.BlockSpec((m, D_MODEL), lambda l, j: (0, 0)),
            pl.BlockSpec((None, D_MODEL, tn), lambda l, j: (l, 0, j)),
            pl.BlockSpec((None, 1, tn), lambda l, j: (l, 0, j)),
        ],
        out_specs=pl.BlockSpec((None, m, tn), lambda l, j: (l, 0, j)),
        out_shape=jax.ShapeDtypeStruct((DEPTH, m, n), F32),
        compiler_params=_params(("arbitrary", "arbitrary")),
        name="ada",
    )(c_all, w_ada, b_ada.reshape(DEPTH, 1, n))


def _mod_spec(mod, chunk, tiles_per_mod):
    r = mod.shape[1]
    return pl.BlockSpec((None, r, D_MODEL), lambda i: (i // tiles_per_mod, 0, chunk))


def _in_kernel(x_ref, sh_ref, sc_ref, g_ref, w_ref, *out_refs):
    h = _rms(x_ref[...]) * g_ref[...]
    h = (h * (1.0 + sc_ref[...]) + sh_ref[...]).astype(BF16)
    off = 0
    for ref in out_refs:
        n = ref.shape[-1]
        ref[...] = _dot(h, w_ref[:, off:off + n])
        off += n


def _in_proj(x, mod, tiles_per_mod, norm_g, w_p):
    t = x.shape[0]
    tm = TOKEN_TILE
    return pl.pallas_call(
        _in_kernel,
        grid=(t // tm,),
        in_specs=[
            pl.BlockSpec((tm, D_MODEL), lambda i: (i, 0)),
            _mod_spec(mod, 0, tiles_per_mod),
            _mod_spec(mod, 1, tiles_per_mod),
            pl.BlockSpec((1, D_MODEL), lambda i: (0, 0)),
            pl.BlockSpec((D_MODEL, IN_PAD_W), lambda i: (0, 0)),
        ],
        out_specs=[pl.BlockSpec((tm, n), lambda i: (i, 0)) for n in IN_SPLITS],
        out_shape=[jax.ShapeDtypeStruct((t, n), F32) for n in IN_SPLITS],
        compiler_params=_params(("arbitrary",)),
        name="in_proj",
    )(x, mod, mod, norm_g, w_p)


def _inv_unit_lower(lm, nfac):
    c = lm.shape[0]
    r = lax.broadcasted_iota(jnp.int32, (c, c), 0)
    q = lax.broadcasted_iota(jnp.int32, (c, c), 1)
    eye = jnp.where(r == q, 1.0, 0.0)
    t = eye - lm
    p = lm
    for _ in range(nfac - 1):
        pb = p.astype(BF16)
        p = _dot(pb, pb)
        t = t + _dot(t.astype(BF16), p.astype(BF16))
    th, tl = _split2(t)
    lh, ll = _split2(lm)
    e = (eye - t) - (_dot(lh, th) + (_dot(lh, tl) + _dot(ll, th)))
    return t + _dot(th, e.astype(BF16))


GDN_STACK = GDN_HEADS * GDN_CHUNK
GDN_PREP_CHUNKS = 2


def _gdn_prep_kernel(qkv_ref, hist_ref, ab_ref, cw_ref, alog_ref, dtb_ref,
                     u_ref, wq_ref, kdt_ref, qk_ref, egl_ref, xs_ref,
                     *, blk, lv, nfac, chunks_per_seq):
    c = GDN_CHUNK
    hc = GDN_STACK
    cb = GDN_PREP_CHUNKS
    hist = SUBLANES
    sh = blk.bit_length() - 1
    first = ((pl.program_id(0) * cb) % chunks_per_seq) == 0

    @pl.when(first)
    def _():
        xs_ref[0:hist, :] = jnp.zeros((hist, CONV_DIM), F32)

    @pl.when(jnp.logical_not(first))
    def _():
        xs_ref[0:hist, :] = hist_ref[...]

    xs_ref[hist:hist + cb * c, :] = qkv_ref[...]

    ri = lax.broadcasted_iota(jnp.int32, (hc, hc), 0)
    cj = lax.broadcasted_iota(jnp.int32, (hc, hc), 1)
    same = (ri >> sh) == (cj >> sh)
    incl = same & (ri >= cj)
    strict = same & (ri > cj)
    rt = lax.broadcasted_iota(jnp.int32, (c, c), 0)
    ct = lax.broadcasted_iota(jnp.int32, (c, c), 1)
    same_t = (rt >> sh) == (ct >> sh)
    cum_t = jnp.where(same_t & (rt >= ct), 1.0, 0.0)
    tot_t = jnp.where(same_t, 1.0, 0.0)
    if lv < blk:
        rvalid = (lax.broadcasted_iota(jnp.int32, (c, 1), 0) & (blk - 1)) >= (blk - lv)

    def stack_cols(a, lane0):
        return jnp.concatenate([a[:, lane0 + h:lane0 + h + 1] for h in range(GDN_HEADS)], axis=0)

    for ci in range(cb):
        base = hist + ci * c

        def conv(c0):
            y = xs_ref[base - 3:base - 3 + c, c0:c0 + LANES] * cw_ref[0:1, c0:c0 + LANES]
            for i in range(1, CONV_W):
                y = y + xs_ref[base - 3 + i:base - 3 + i + c, c0:c0 + LANES] * cw_ref[i:i + 1, c0:c0 + LANES]
            y = _silu(y)
            return jnp.where(rvalid, y, 0.0) if lv < blk else y

        ab = ab_ref[ci * c:(ci + 1) * c, :]
        g_all = -jnp.exp(alog_ref[...]) * _softplus(ab + dtb_ref[...])
        beta_all = jax.nn.sigmoid(ab)
        if lv < blk:
            g_all = jnp.where(rvalid, g_all, 0.0)
            beta_all = jnp.where(rvalid, beta_all, 0.0)
        gcum = _dot_exact_lhs(cum_t, g_all)
        gtot = _dot_exact_lhs(tot_t, g_all)
        gc = stack_cols(gcum, 0)
        gl = stack_cols(gtot, 0)
        beta = stack_cols(beta_all, GDN_HEADS)
        gcb = jnp.broadcast_to(gc, (hc, LANES))
        gr = jnp.concatenate([gcb[0:LANES].T[0:1], gcb[LANES:].T[0:1]], axis=1)

        q = jnp.concatenate([conv(h * GDN_DK) for h in range(GDN_HEADS)], axis=0)
        k = jnp.concatenate([conv(GDN_QK_W + h * GDN_DK) for h in range(GDN_HEADS)], axis=0)
        v = jnp.concatenate([conv(2 * GDN_QK_W + h * GDN_DV) for h in range(GDN_HEADS)], axis=0)
        q = q * lax.rsqrt(jnp.sum(q * q, axis=-1, keepdims=True) + EPS) * (GDN_DK ** -0.5)
        k = k * lax.rsqrt(jnp.sum(k * k, axis=-1, keepdims=True) + EPS)

        decay = jnp.where(incl, jnp.exp(jnp.minimum(gc - gr, 0.0)), 0.0)
        kb = k * beta
        k16 = k.astype(BF16)
        lm = jnp.where(strict, _dot_nt(kb.astype(BF16), k16) * decay, 0.0)
        tinv = _inv_unit_lower(lm, nfac)
        eg = jnp.exp(gc)
        rh, rl = _split2(jnp.concatenate([v * beta, kb * eg], axis=1))
        t16 = tinv.astype(BF16)
        x = _dot(t16, rh) + _dot(t16, rl)
        w = x[:, GDN_DV:]
        qd = q * eg
        kd = k * jnp.exp(gl - gc)
        qk_ref[ci] = (_dot_nt(q.astype(BF16), k16) * decay).astype(BF16)
        kdt_ref[ci] = kd.T.astype(BF16)
        for h in range(GDN_HEADS):
            sl = slice(h * c, (h + 1) * c)
            u_ref[ci * c:(ci + 1) * c, h * GDN_DV:(h + 1) * GDN_DV] = x[sl, :GDN_DV]
            wq_ref[ci, h] = jnp.concatenate([w[sl], qd[sl]], axis=0).astype(BF16)
            egl = jnp.exp(jnp.broadcast_to(gtot[:, h:h + 1], (c, LANES)))
            egl_ref[ci, h] = egl.reshape(c // SUBLANES, SUBLANES, LANES)[:, 0, :]


def _gdn_scan_kernel(u_ref, wq_ref, kdt_ref, qk_ref, egl_ref, s0_ref, o_ref, s_ref, *, blk, sb):
    c = GDN_CHUNK
    nb = c // blk
    sh = blk.bit_length() - 1
    n = pl.program_id(1)

    @pl.when(n == 0)
    def _():
        s_ref[...] = s0_ref[...]

    rblk2 = (lax.broadcasted_iota(jnp.int32, (2 * c, 1), 0) & (c - 1)) >> sh
    rblk_s = (lax.broadcasted_iota(jnp.int32, (GDN_STACK, 1), 0) & (c - 1)) >> sh
    for b in range(sb):
        vn, qs = [], []
        for h in range(GDN_HEADS):
            wq = wq_ref[b, h]
            rr = _dot(wq, s_ref[b, 0, h].astype(BF16))
            for j in range(1, nb):
                rr = jnp.where(rblk2 == j, _dot(wq, s_ref[b, j, h].astype(BF16)), rr)
            vn.append(u_ref[b, :, h * GDN_DV:(h + 1) * GDN_DV] - rr[:c])
            qs.append(rr[c:])
        vnew = jnp.concatenate(vn, axis=0)
        o = jnp.concatenate(qs, axis=0) + _dot(qk_ref[b], vnew.astype(BF16))
        for h in range(GDN_HEADS):
            o_ref[b, :, h * GDN_DV:(h + 1) * GDN_DV] = o[h * c:(h + 1) * c]
        for j in range(nb):
            vj = (vnew if nb == 1 else jnp.where(rblk_s == j, vnew, 0.0)).astype(BF16)
            zero = jnp.zeros((c, GDN_DV), BF16)
            vbd = jnp.concatenate(
                [jnp.concatenate([vj[h * c:(h + 1) * c] if g == h else zero for g in range(GDN_HEADS)], axis=1)
                 for h in range(GDN_HEADS)], axis=0)
            upd = _dot(kdt_ref[b], vbd)
            for h in range(GDN_HEADS):
                s_ref[b, j, h] = (egl_ref[b, h, j:j + 1, :] * s_ref[b, j, h]
                                  + upd[:, h * GDN_DV:(h + 1) * GDN_DV])


def _gdn(qkv, ab, s0, conv_w, alog_row, dtb_row, *, blk, lv, chunks_per_seq, sb):
    rows = qkv.shape[0]
    c = GDN_CHUNK
    hc = GDN_STACK
    cb = GDN_PREP_CHUNKS
    nb = c // blk
    assert blk in (SUBLANES, c) and lv <= blk
    nfac = max(1, (lv - 1).bit_length())
    nch = rows // c
    nseq = nch // chunks_per_seq
    assert nseq * nb == s0.shape[0] and nch % cb == 0
    assert chunks_per_seq == 1 or chunks_per_seq % cb == 0
    hist_blocks = cb * c // SUBLANES
    prep = functools.partial(_gdn_prep_kernel, blk=blk, lv=lv, nfac=nfac, chunks_per_seq=chunks_per_seq)
    u, wq, kdt, qk, egl = pl.pallas_call(
        prep,
        grid=(nch // cb,),
        in_specs=[
            pl.BlockSpec((cb * c, CONV_DIM), lambda i: (i, 0)),
            pl.BlockSpec((SUBLANES, CONV_DIM), lambda i: (jnp.maximum(i * hist_blocks - 1, 0), 0)),
            pl.BlockSpec((cb * c, LANES), lambda i: (i, 0)),
            pl.BlockSpec((CONV_W, CONV_DIM), lambda i: (0, 0)),
            pl.BlockSpec((1, LANES), lambda i: (0, 0)),
            pl.BlockSpec((1, LANES), lambda i: (0, 0)),
        ],
        out_specs=[
            pl.BlockSpec((cb * c, GDN_V_W), lambda i: (i, 0)),
            pl.BlockSpec((cb, GDN_HEADS, 2 * c, GDN_DK), lambda i: (i, 0, 0, 0)),
            pl.BlockSpec((cb, GDN_DK, hc), lambda i: (i, 0, 0)),
            pl.BlockSpec((cb, hc, hc), lambda i: (i, 0, 0)),
            pl.BlockSpec((cb, GDN_HEADS, SUBLANES, LANES), lambda i: (i, 0, 0, 0)),
        ],
        out_shape=[
            jax.ShapeDtypeStruct((rows, GDN_V_W), F32),
            jax.ShapeDtypeStruct((nch, GDN_HEADS, 2 * c, GDN_DK), BF16),
            jax.ShapeDtypeStruct((nch, GDN_DK, hc), BF16),
            jax.ShapeDtypeStruct((nch, hc, hc), BF16),
            jax.ShapeDtypeStruct((nch, GDN_HEADS, SUBLANES, LANES), F32),
        ],
        scratch_shapes=[pltpu.VMEM((SUBLANES + cb * c, CONV_DIM), F32)],
        compiler_params=_params(("arbitrary",)),
        name="gdn_prep",
    )(qkv, qkv, ab, conv_w, alog_row, dtb_row)

    n = chunks_per_seq
    scan = functools.partial(_gdn_scan_kernel, blk=blk, sb=sb)
    seq5 = lambda a: a.reshape((nseq, n) + a.shape[1:])
    s0g = s0.reshape(nseq, nb, GDN_HEADS, GDN_DK, GDN_DV)
    o, s_new = pl.pallas_call(
        scan,
        grid=(nseq // sb, n),
        in_specs=[
            pl.BlockSpec((sb, c, GDN_V_W), lambda i, t: (i, t, 0)),
            pl.BlockSpec((sb, None, GDN_HEADS, 2 * c, GDN_DK), lambda i, t: (i, t, 0, 0, 0)),
            pl.BlockSpec((sb, None, GDN_DK, hc), lambda i, t: (i, t, 0, 0)),
            pl.BlockSpec((sb, None, hc, hc), lambda i, t: (i, t, 0, 0)),
            pl.BlockSpec((sb, None, GDN_HEADS, SUBLANES, LANES), lambda i, t: (i, t, 0, 0, 0)),
            pl.BlockSpec((sb, nb, GDN_HEADS, GDN_DK, GDN_DV), lambda i, t: (i, 0, 0, 0, 0)),
        ],
        out_specs=[
            pl.BlockSpec((sb, c, GDN_V_W), lambda i, t: (i, t, 0)),
            pl.BlockSpec((sb, nb, GDN_HEADS, GDN_DK, GDN_DV), lambda i, t: (i, 0, 0, 0, 0)),
        ],
        out_shape=[
            jax.ShapeDtypeStruct((nseq, n * c, GDN_V_W), F32),
            jax.ShapeDtypeStruct(s0g.shape, F32),
        ],
        compiler_params=_params(("arbitrary", "arbitrary")),
        name="gdn_scan",
    )(u.reshape(nseq, n * c, GDN_V_W), seq5(wq), seq5(kdt), seq5(qk), seq5(egl), s0g)
    return o.reshape(rows, GDN_V_W), s_new.reshape(s0.shape)


def _swa_kernel(q_ref, kc_ref, vc_ref, kp_ref, vp_ref, sink_ref, o_ref, *, qb, bb, prev_is_cache):
    w = WINDOW
    n = pl.program_id(1)
    rows = SWA_GROUP * qb
    qi = lax.broadcasted_iota(jnp.int32, (rows, 2 * w), 0) & (qb - 1)
    kj = lax.broadcasted_iota(jnp.int32, (rows, 2 * w), 1)
    ok = ((kj < w) & (kj >= qi)) | ((kj >= w) & (kj - w <= qi))
    if not prev_is_cache:
        ok = ok & ((kj >= w) | (n > 0))
    scale = SWA_HEAD_DIM ** -0.5

    def pad_rows(a):
        return a if qb == w else jnp.concatenate([a, jnp.zeros((w - qb, a.shape[1]), a.dtype)], axis=0)

    for b in range(bb):
        r0 = b * qb
        kc_all = pad_rows(kc_ref[r0:r0 + qb, :])
        vc_all = pad_rows(vc_ref[r0:r0 + qb, :])
        kp_all = kp_ref[b] if prev_is_cache else kp_ref[...]
        vp_all = vp_ref[b] if prev_is_cache else vp_ref[...]
        for hk in range(SWA_KV_HEADS):
            c0 = hk * SWA_HEAD_DIM
            sl = slice(c0, c0 + SWA_HEAD_DIM)
            k_all = jnp.concatenate([kp_all[:, sl], kc_all[:, sl]], axis=0).astype(BF16)
            v_all = jnp.concatenate([vp_all[:, sl], vc_all[:, sl]], axis=0).astype(BF16)
            heads = [hk * SWA_GROUP + g for g in range(SWA_GROUP)]
            q = jnp.concatenate(
                [q_ref[r0:r0 + qb, hd * SWA_HEAD_DIM:(hd + 1) * SWA_HEAD_DIM] for hd in heads],
                axis=0).astype(BF16)
            sink = jnp.concatenate(
                [jnp.broadcast_to(sink_ref[0:1, hd:hd + 1], (qb, 1)) for hd in heads], axis=0)
            s = jnp.where(ok, _dot_nt(q, k_all) * scale, NEG_BIG)
            m = jnp.maximum(jnp.max(s, axis=-1, keepdims=True), sink)
            p = jnp.exp(s - m)
            den = jnp.sum(p, axis=-1, keepdims=True) + jnp.exp(sink - m)
            res = _dot(p.astype(BF16), v_all) / den
            for g, hd in enumerate(heads):
                o_ref[r0:r0 + qb, hd * SWA_HEAD_DIM:(hd + 1) * SWA_HEAD_DIM] = res[g * qb:(g + 1) * qb]


def _swa(q, kv, sink_row, *, qb, bb, blocks_per_seq, cache_k=None, cache_v=None):
    rows = q.shape[0]
    w = WINDOW
    prev_is_cache = cache_k is not None
    steps = rows // (qb * bb)
    nseq = steps // blocks_per_seq
    kern = functools.partial(_swa_kernel, qb=qb, bb=bb, prev_is_cache=prev_is_cache)
    cur = lambda i, n: (i * blocks_per_seq + n, 0)
    if prev_is_cache:
        kp_spec = pl.BlockSpec((bb, w, SWA_KV_W), lambda i, n: (i, 0, 0))
        vp_spec = kp_spec
        kp_arr, vp_arr = cache_k, cache_v
    else:
        assert qb == w and bb == 1
        kp_spec = pl.BlockSpec((w, SWA_KV_W), lambda i, n: (i * blocks_per_seq + jnp.maximum(n - 1, 0), 0))
        vp_spec = pl.BlockSpec((w, SWA_KV_W), lambda i, n: (i * blocks_per_seq + jnp.maximum(n - 1, 0), 1))
        kp_arr, vp_arr = kv, kv
    return pl.pallas_call(
        kern,
        grid=(nseq, blocks_per_seq),
        in_specs=[
            pl.BlockSpec((qb * bb, SWA_Q_W), cur),
            pl.BlockSpec((qb * bb, SWA_KV_W), cur),
            pl.BlockSpec((qb * bb, SWA_KV_W), lambda i, n: (i * blocks_per_seq + n, 1)),
            kp_spec,
            vp_spec,
            pl.BlockSpec((1, LANES), lambda i, n: (0, 0)),
        ],
        out_specs=pl.BlockSpec((qb * bb, SWA_Q_W), cur),
        out_shape=jax.ShapeDtypeStruct((rows, SWA_Q_W), F32),
        compiler_params=_params(("arbitrary", "arbitrary")),
        name="swa",
    )(q, kv, kv, kp_arr, vp_arr, sink_row)


def _mix_kernel(o_ref, z_ref, ob_ref, gates_ref, x_ref, gt_ref, gn_ref, wb_ref, wo_ref, y_ref):
    parts = []
    for h in range(GDN_HEADS):
        sl = slice(h * GDN_DV, (h + 1) * GDN_DV)
        parts.append(_rms(o_ref[:, sl]) * gn_ref[...] * _silu(z_ref[:, sl]))
    oa = jnp.concatenate(parts, axis=1).astype(BF16)
    pa = _dot(oa, wb_ref[0:GDN_V_W, :])
    pb = _dot(ob_ref[...].astype(BF16), wb_ref[GDN_V_W:, :])
    mixed = (jax.nn.sigmoid(gates_ref[:, 0:D_MODEL]) * pa
             + jax.nn.sigmoid(gates_ref[:, D_MODEL:]) * pb)
    y_ref[...] = x_ref[...] + gt_ref[...] * _dot(mixed.astype(BF16), wo_ref[...])


def _mix(o_raw, z, ob, gates, x, mod, tiles_per_mod, gn_row, wb, wo):
    t = x.shape[0]
    tm = TOKEN_TILE
    row = lambda n: pl.BlockSpec((tm, n), lambda i: (i, 0))
    full = lambda a: pl.BlockSpec(a.shape, lambda i: (0, 0))
    return pl.pallas_call(
        _mix_kernel,
        grid=(t // tm,),
        in_specs=[row(GDN_V_W), row(GDN_V_W), row(SWA_Q_W), row(2 * D_MODEL), row(D_MODEL),
                  _mod_spec(mod, 2, tiles_per_mod), full(gn_row), full(wb), full(wo)],
        out_specs=row(D_MODEL),
        out_shape=jax.ShapeDtypeStruct((t, D_MODEL), F32),
        compiler_params=_params(("arbitrary",)),
        name="mix",
    )(o_raw, z, ob, gates, x, mod, gn_row, wb, wo)


def _ffn_kernel(x_ref, sh_ref, sc_ref, gt_ref, g_ref, wg_ref, wu_ref, wd_ref, y_ref, h_ref, acc_ref):
    j = pl.program_id(1)

    @pl.when(j == 0)
    def _():
        h = _rms(x_ref[...]) * g_ref[...]
        h_ref[...] = (h * (1.0 + sc_ref[...]) + sh_ref[...]).astype(BF16)

    h = h_ref[...]
    a = (_silu(_dot(h, wg_ref[...])) * _dot(h, wu_ref[...])).astype(BF16)
    d = _dot(a, wd_ref[...])

    @pl.when(j == 0)
    def _():
        acc_ref[...] = d

    @pl.when(j > 0)
    def _():
        acc_ref[...] += d

    @pl.when(j == pl.num_programs(1) - 1)
    def _():
        y_ref[...] = x_ref[...] + gt_ref[...] * acc_ref[...]


def _ffn(x, mod, tiles_per_mod, norm_g, wg, wu, wd):
    t = x.shape[0]
    tm = 2 * TOKEN_TILE
    tpm = max(1, tiles_per_mod // 2)
    d_ff = wg.shape[1]
    tf = FFN_FF_TILE
    r = mod.shape[1]
    mspec = lambda chunk: pl.BlockSpec((None, r, D_MODEL), lambda i, j: (i // tpm, 0, chunk))
    return pl.pallas_call(
        _ffn_kernel,
        grid=(t // tm, d_ff // tf),
        in_specs=[
            pl.BlockSpec((tm, D_MODEL), lambda i, j: (i, 0)),
            mspec(3), mspec(4), mspec(5),
            pl.BlockSpec((1, D_MODEL), lambda i, j: (0, 0)),
            pl.BlockSpec((D_MODEL, tf), lambda i, j: (0, j)),
            pl.BlockSpec((D_MODEL, tf), lambda i, j: (0, j)),
            pl.BlockSpec((tf, D_MODEL), lambda i, j: (j, 0)),
        ],
        out_specs=pl.BlockSpec((tm, D_MODEL), lambda i, j: (i, 0)),
        out_shape=jax.ShapeDtypeStruct((t, D_MODEL), F32),
        scratch_shapes=[pltpu.VMEM((tm, D_MODEL), BF16), pltpu.VMEM((tm, D_MODEL), F32)],
        compiler_params=_params(("arbitrary", "arbitrary")),
        name="ffn",
    )(x, mod, mod, mod, norm_g, wg, wu, wd)


def _route_kernel(x_ref, sh_ref, sc_ref, g_ref, wr_ref, h_ref, idx_ref, wt_ref):
    h = _rms(x_ref[...]) * g_ref[...]
    h = h * (1.0 + sc_ref[...]) + sh_ref[...]
    h_ref[...] = h.astype(BF16)
    logits = _dot3(h, wr_ref[...])
    lane_i = lax.broadcasted_iota(jnp.int32, logits.shape, 1)
    lane = lane_i.astype(F32)
    logits = jnp.where(lane_i < N_EXPERTS, logits, NEG_BIG)
    m1 = jnp.max(logits, axis=-1, keepdims=True)
    i1 = jnp.min(jnp.where(logits == m1, lane, float(LANES)), axis=-1, keepdims=True)
    rest = jnp.where(lane == i1, NEG_BIG, logits)
    m2 = jnp.max(rest, axis=-1, keepdims=True)
    i2 = jnp.min(jnp.where(rest == m2, lane, float(LANES)), axis=-1, keepdims=True)
    e = jnp.exp(m2 - m1)
    den = 1.0 + e
    idx_ref[...] = jnp.where(lane_i == 0, i1, jnp.where(lane_i == 1, i2, 0.0)).astype(jnp.int32)
    wt_ref[...] = jnp.where(lane_i == 0, 1.0 / den, jnp.where(lane_i == 1, e / den, 0.0))


def _route(x, mod, tiles_per_mod, norm_g, wr_pad):
    t = x.shape[0]
    tm = TOKEN_TILE
    row = lambda n: pl.BlockSpec((tm, n), lambda i: (i, 0))
    return pl.pallas_call(
        _route_kernel,
        grid=(t // tm,),
        in_specs=[row(D_MODEL), _mod_spec(mod, 3, tiles_per_mod), _mod_spec(mod, 4, tiles_per_mod),
                  pl.BlockSpec((1, D_MODEL), lambda i: (0, 0)),
                  pl.BlockSpec((D_MODEL, LANES), lambda i: (0, 0))],
        out_specs=[row(D_MODEL), row(LANES), row(LANES)],
        out_shape=[jax.ShapeDtypeStruct((t, D_MODEL), BF16),
                   jax.ShapeDtypeStruct((t, LANES), jnp.int32),
                   jax.ShapeDtypeStruct((t, LANES), F32)],
        compiler_params=_params(("arbitrary",)),
        name="route",
    )(x, mod, mod, norm_g, wr_pad)


def _moe_kernel(te_ref, nu_ref, x_ref, rw_ref, wg_ref, wu_ref, wd_ref, o_ref):
    i = pl.program_id(0)
    j = pl.program_id(1)
    used = i < nu_ref[0]

    @pl.when(used)
    def _():
        x = x_ref[...]
        a = (_silu(_dot(x, wg_ref[...].astype(BF16))) * _dot(x, wu_ref[...].astype(BF16))).astype(BF16)
        d = _dot(a, wd_ref[...].astype(BF16))

        @pl.when(j == 0)
        def _():
            o_ref[...] = d

        @pl.when(j > 0)
        def _():
            o_ref[...] += d

        @pl.when(j == pl.num_programs(1) - 1)
        def _():
            o_ref[...] = o_ref[...] * rw_ref[...]

    @pl.when(jnp.logical_not(used) & (j == 0))
    def _():
        o_ref[...] = jnp.zeros_like(o_ref)


def _moe(xs, row_w, tile_expert, n_used, wg, wu, wd):
    p = xs.shape[0]
    tm = MOE_TILE
    tf = MOE_FF_TILE
    d_ff = wg.shape[2]
    nj = d_ff // tf

    def jj(i, j, nu):
        return jnp.where(i < nu[0], j, nj - 1)

    grid_spec = pltpu.PrefetchScalarGridSpec(
        num_scalar_prefetch=2,
        grid=(p // tm, nj),
        in_specs=[
            pl.BlockSpec((tm, D_MODEL), lambda i, j, te, nu: (i, 0)),
            pl.BlockSpec((tm, 1), lambda i, j, te, nu: (i, 0)),
            pl.BlockSpec((None, D_MODEL, tf), lambda i, j, te, nu: (te[i], 0, jj(i, j, nu))),
            pl.BlockSpec((None, D_MODEL, tf), lambda i, j, te, nu: (te[i], 0, jj(i, j, nu))),
            pl.BlockSpec((None, tf, D_MODEL), lambda i, j, te, nu: (te[i], jj(i, j, nu), 0)),
        ],
        out_specs=pl.BlockSpec((tm, D_MODEL), lambda i, j, te, nu: (i, 0)),
    )
    return pl.pallas_call(
        _moe_kernel,
        grid_spec=grid_spec,
        out_shape=jax.ShapeDtypeStruct((p, D_MODEL), F32),
        compiler_params=_params(("arbitrary", "arbitrary")),
        name="moe",
    )(tile_expert, n_used, xs, row_w, wg, wu, wd)


def _dispatch(top_i, top_w):
    t = top_i.shape[0]
    tm = MOE_TILE
    n_assign = t * TOP_K
    n_tiles = -(-n_assign // tm) + N_EXPERTS
    e_flat = top_i.reshape(-1)
    onehot = (e_flat[:, None] == jnp.arange(N_EXPERTS, dtype=jnp.int32)[None, :]).astype(jnp.int32)
    csum = jnp.cumsum(onehot, axis=0)
    counts = csum[-1]
    rank = jnp.take_along_axis(csum, e_flat[:, None], axis=1)[:, 0] - 1
    tiles_e = (counts + tm - 1) // tm
    tile_end = jnp.cumsum(tiles_e)
    tile_start = tile_end - tiles_e
    pos = tile_start[e_flat] * tm + rank
    n_used = tile_end[-1]
    tile_ids = jnp.arange(n_tiles, dtype=jnp.int32)
    tile_expert = jnp.sum((tile_ids[:, None] >= tile_end[None, :]).astype(jnp.int32), axis=1)
    last_e = jnp.max(jnp.where(tiles_e > 0, jnp.arange(N_EXPERTS, dtype=jnp.int32), 0))
    tile_expert = jnp.where(tile_ids < n_used, tile_expert, last_e).astype(jnp.int32)
    p = n_tiles * tm
    tok = jnp.arange(n_assign, dtype=jnp.int32) // TOP_K
    row_token = jnp.zeros((p,), jnp.int32).at[pos].set(tok)
    row_w = jnp.zeros((p,), F32).at[pos].set(top_w.reshape(-1))
    return pos.reshape(t, TOP_K), row_token, row_w[:, None], tile_expert, n_used.reshape(1).astype(jnp.int32)


def _combine_kernel(x_ref, ya_ref, yb_ref, gt_ref, g_ref, y_ref):
    x = x_ref[...] + gt_ref[...] * (ya_ref[...] + yb_ref[...])
    y_ref[...] = _rms(x) * g_ref[...]


def _combine(x, ya, yb, mod, tiles_per_mod, norm_g):
    t = x.shape[0]
    tm = TOKEN_TILE
    row = pl.BlockSpec((tm, D_MODEL), lambda i: (i, 0))
    return pl.pallas_call(
        _combine_kernel,
        grid=(t // tm,),
        in_specs=[row, row, row, _mod_spec(mod, 5, tiles_per_mod),
                  pl.BlockSpec((1, D_MODEL), lambda i: (0, 0))],
        out_specs=row,
        out_shape=jax.ShapeDtypeStruct((t, D_MODEL), F32),
        compiler_params=_params(("arbitrary",)),
        name="combine",
    )(x, ya, yb, mod, norm_g)


def _lane_row(vals):
    return jnp.zeros((1, LANES), F32).at[0, :vals.shape[0]].set(vals.astype(F32))


def _permute_w_in(w):
    c = [CONV_DIM, CONV_DIM + GDN_V_W]
    a0 = c[1]
    q0 = a0 + 2 * GDN_HEADS
    k0 = q0 + SWA_Q_W
    g0 = k0 + 2 * SWA_KV_W
    cols = [w[:, :c[0]], w[:, c[0]:c[1]], w[:, q0:k0], w[:, k0:g0], w[:, g0:],
            w[:, a0:q0], jnp.zeros((D_MODEL, LANES - 2 * GDN_HEADS), w.dtype)]
    return jnp.concatenate(cols, axis=1).astype(BF16)


def kernel(x_prompt, x_sample, c_prompt, c_sample, state_gdn, state_conv, cache_win_k, cache_win_v, w_in, conv_w, a_log, dt_bias, gdn_norm, attn_sinks, w_branch, w_out, w_ada, b_ada, norm_mix, norm_ffn, w_gate_dense, w_up_dense, w_down_dense, w_router, w_gate_moe, w_up_moe, w_down_moe, final_norm):
    bp, lp, _ = x_prompt.shape
    bs, ls, _ = x_sample.shape
    tp, ts = bp * lp, bs * ls
    tm = TOKEN_TILE
    pad_rows = SUBLANES - ls

    m_pad = -(-(bp + bs) // SUBLANES) * SUBLANES
    c_all = jnp.concatenate([c_prompt, c_sample, jnp.zeros((m_pad - bp - bs, D_MODEL), F32)], axis=0)
    ada = _ada(c_all, w_ada, b_ada)

    xp = x_prompt.reshape(tp, D_MODEL)
    xs = x_sample.reshape(ts, D_MODEL)
    outs = {k: [] for k in ("gdn_p", "conv_p", "k_p", "v_p", "gdn_s", "conv_s", "k_s", "v_s")}
    for l in range(DEPTH):
        mod_p = ada[l, :bp].reshape(bp, 1, 6 * D_MODEL)
        mod_s = jnp.repeat(ada[l, bp:bp + bs], ls, axis=0).reshape(ts // tm, tm, 6 * D_MODEL)
        tpm_p = lp // tm
        w_p = _permute_w_in(w_in[l])
        nmix = norm_mix[l].reshape(1, D_MODEL)
        nffn = norm_ffn[l].reshape(1, D_MODEL)
        alog_row, dtb_row = _lane_row(a_log[l]), _lane_row(dt_bias[l])
        sink_row = _lane_row(attn_sinks[l])
        gn_row = gdn_norm[l].reshape(1, GDN_DV)
        wb = w_branch[l].astype(BF16)
        wo = w_out[l].astype(BF16)

        qkv_p, z_p, qb_p, kv_p, gates_p, ab_p = _in_proj(xp, mod_p, tpm_p, nmix, w_p)
        s0_p = jnp.zeros((bp, GDN_HEADS, GDN_DK, GDN_DV), F32)
        o_p, s_p = _gdn(qkv_p, ab_p, s0_p, conv_w[l], alog_row, dtb_row,
                        blk=GDN_CHUNK, lv=GDN_CHUNK, chunks_per_seq=lp // GDN_CHUNK, sb=bp)
        ob_p = _swa(qb_p, kv_p, sink_row, qb=WINDOW, bb=1, blocks_per_seq=lp // WINDOW)
        xp = _mix(o_p, z_p, ob_p, gates_p, xp, mod_p, tpm_p, gn_row, wb, wo)
        outs["gdn_p"].append(s_p)
        outs["conv_p"].append(qkv_p.reshape(bp, lp, CONV_DIM)[:, lp - (CONV_W - 1):])
        kv3 = kv_p.reshape(bp, lp, 2 * SWA_KV_W)[:, lp - WINDOW:]
        outs["k_p"].append(kv3[..., :SWA_KV_W].reshape(bp, WINDOW, SWA_KV_HEADS, SWA_HEAD_DIM))
        outs["v_p"].append(kv3[..., SWA_KV_W:].reshape(bp, WINDOW, SWA_KV_HEADS, SWA_HEAD_DIM))

        qkv_s, z_s, qb_s, kv_s, gates_s, ab_s = _in_proj(xs, mod_s, 1, nmix, w_p)
        qkv_s3 = qkv_s.reshape(bs, ls, CONV_DIM)
        hist = jnp.concatenate([jnp.zeros((bs, SUBLANES - ls - (CONV_W - 1), CONV_DIM), F32),
                                state_conv[l], qkv_s3], axis=1)
        ab_blk = jnp.concatenate([jnp.zeros((bs, pad_rows, LANES), F32), ab_s.reshape(bs, ls, LANES)], axis=1)
        o_s, s_s = _gdn(hist.reshape(bs * SUBLANES, CONV_DIM), ab_blk.reshape(bs * SUBLANES, LANES),
                        state_gdn[l], conv_w[l], alog_row, dtb_row,
                        blk=SUBLANES, lv=ls, chunks_per_seq=1, sb=2)
        o_s = o_s.reshape(bs, SUBLANES, GDN_V_W)[:, pad_rows:].reshape(ts, GDN_V_W)
        pad_after = lambda a: jnp.concatenate(
            [a.reshape(bs, ls, -1), jnp.zeros((bs, pad_rows, a.shape[-1]), F32)], axis=1).reshape(bs * SUBLANES, -1)
        ck = cache_win_k[l].reshape(bs, WINDOW, SWA_KV_W)
        cv = cache_win_v[l].reshape(bs, WINDOW, SWA_KV_W)
        ob_s = _swa(pad_after(qb_s), pad_after(kv_s), sink_row, qb=SUBLANES, bb=8, blocks_per_seq=1,
                    cache_k=ck, cache_v=cv)
        ob_s = ob_s.reshape(bs, SUBLANES, SWA_Q_W)[:, :ls].reshape(ts, SWA_Q_W)
        xs = _mix(o_s, z_s, ob_s, gates_s, xs, mod_s, 1, gn_row, wb, wo)
        outs["gdn_s"].append(s_s)
        outs["conv_s"].append(jnp.concatenate([state_conv[l], qkv_s3], axis=1)[:, -(CONV_W - 1):])
        kv_s3 = kv_s.reshape(bs, ls, 2 * SWA_KV_W)
        knew = jnp.concatenate([ck, kv_s3[..., :SWA_KV_W]], axis=1)[:, -WINDOW:]
        vnew = jnp.concatenate([cv, kv_s3[..., SWA_KV_W:]], axis=1)[:, -WINDOW:]
        outs["k_s"].append(knew.reshape(bs, WINDOW, SWA_KV_HEADS, SWA_HEAD_DIM))
        outs["v_s"].append(vnew.reshape(bs, WINDOW, SWA_KV_HEADS, SWA_HEAD_DIM))

        i = l // 2
        if l % 2 == 0:
            wg, wu, wd = (w_gate_dense[i].astype(BF16), w_up_dense[i].astype(BF16),
                          w_down_dense[i].astype(BF16))
            xp = _ffn(xp, mod_p, tpm_p, nffn, wg, wu, wd)
            xs = _ffn(xs, mod_s.reshape(ts // (2 * tm), 2 * tm, 6 * D_MODEL), 1, nffn, wg, wu, wd)
        else:
            wr_pad = jnp.concatenate([w_router[i], jnp.zeros((D_MODEL, LANES - N_EXPERTS), F32)], axis=1)
            h_p, idx_p, wt_p = _route(xp, mod_p, tpm_p, nffn, wr_pad)
            h_s, idx_s, wt_s = _route(xs, mod_s, 1, nffn, wr_pad)
            h_all = jnp.concatenate([h_p, h_s], axis=0)
            top_i = jnp.concatenate([idx_p[:, :TOP_K], idx_s[:, :TOP_K]], axis=0)
            top_w = jnp.concatenate([wt_p[:, :TOP_K], wt_s[:, :TOP_K]], axis=0)
            pos, row_token, row_w, tile_expert, n_used = _dispatch(top_i, top_w)
            y = _moe(jnp.take(h_all, row_token, axis=0), row_w, tile_expert, n_used,
                     w_gate_moe[i], w_up_moe[i], w_down_moe[i])
            ya = jnp.take(y, pos[:, 0], axis=0)
            yb = jnp.take(y, pos[:, 1], axis=0)
            fin = final_norm.reshape(1, D_MODEL)
            assert l == DEPTH - 1
            xp = _combine(xp, ya[:tp], yb[:tp], mod_p, tpm_p, fin)
            xs = _combine(xs, ya[tp:], yb[tp:], mod_s, 1, fin)

    st = lambda k: jnp.stack(outs[k])
    return (xp.reshape(bp, lp, D_MODEL), xs.reshape(bs, ls, D_MODEL),
            st("gdn_p"), st("conv_p"), st("k_p"), st("v_p"),
            st("gdn_s"), st("conv_s"), st("k_s"), st("v_s"))
```

```python
import functools

import jax
import jax.numpy as jnp
from jax import lax
from jax.experimental import pallas as pl
from jax.experimental.pallas import tpu as pltpu

F32 = jnp.float32
BF16 = jnp.bfloat16

D_MODEL = 1024
DEPTH = 2
GDN_HEADS = 4
GDN_DK = 128
GDN_DV = 128
CONV_W = 4
SWA_HEADS = 8
SWA_KV_HEADS = 2
SWA_HEAD_DIM = 64
SWA_GROUP = SWA_HEADS // SWA_KV_HEADS
WINDOW = 128
GDN_QK_W = GDN_HEADS * GDN_DK
GDN_V_W = GDN_HEADS * GDN_DV
CONV_DIM = 2 * GDN_QK_W + GDN_V_W
SWA_Q_W = SWA_HEADS * SWA_HEAD_DIM
SWA_KV_W = SWA_KV_HEADS * SWA_HEAD_DIM
N_EXPERTS = 8
TOP_K = 2
EPS = 1e-6

LANES = 128
SUBLANES = 8
VMEM_LIMIT = 56 * 1024 * 1024
NEG_BIG = -1e30

GDN_CHUNK = 64
TOKEN_TILE = 256
MOE_TILE = 1024
MOE_FF_TILE = 512
FFN_FF_TILE = 1408

IN_SPLITS = (CONV_DIM, GDN_V_W, SWA_Q_W, 2 * SWA_KV_W, 2 * D_MODEL, LANES)
IN_PAD_W = sum(IN_SPLITS)


def _silu(x):
    return x * jax.nn.sigmoid(x)


def _softplus(x):
    return jnp.maximum(x, 0.0) + jnp.log1p(jnp.exp(-jnp.abs(x)))


def _dot(a, b):
    return jnp.dot(a, b, preferred_element_type=F32)


def _dot_nt(a, b):
    return lax.dot_general(a, b, (((1,), (1,)), ((), ())), preferred_element_type=F32)


def _dot_tn(a, b):
    return lax.dot_general(a, b, (((0,), (0,)), ((), ())), preferred_element_type=F32)


def _split2(a):
    hi = a.astype(BF16)
    lo = (a - hi.astype(F32)).astype(BF16)
    return hi, lo


def _split3(a):
    hi = a.astype(BF16)
    r = a - hi.astype(F32)
    mid = r.astype(BF16)
    lo = (r - mid.astype(F32)).astype(BF16)
    return hi, mid, lo


def _dot3(a, b):
    ah, al = _split2(a)
    bh, bl = _split2(b)
    return _dot(ah, bh) + (_dot(ah, bl) + _dot(al, bh))


def _dot_exact_lhs(a01, b):
    a = a01.astype(BF16)
    bh, bm, bl = _split3(b)
    return _dot(a, bh) + (_dot(a, bm) + _dot(a, bl))


def _rms(x):
    return x * lax.rsqrt(jnp.mean(x * x, axis=-1, keepdims=True) + EPS)


def _params(sem):
    return pltpu.CompilerParams(dimension_semantics=sem, vmem_limit_bytes=VMEM_LIMIT)


def _ada_kernel(c_ref, w_ref, b_ref, o_ref):
    o_ref[...] = _dot3(_silu(c_ref[...]), w_ref[...]) + b_ref[...]


def _ada(c_all, w_ada, b_ada):
    m = c_all.shape[0]
    tn = 1536
    n = 6 * D_MODEL
    return pl.pallas_call(
        _ada_kernel,
        grid=(DEPTH, n // tn),
        in_specs=[
            pl.BlockSpec((m, D_MODEL), lambda l, j: (0, 0)),
            pl.BlockSpec((None, D_MODEL, tn), lambda l, j: (l, 0, j)),
            pl.BlockSpec((None, 1, tn), lambda l, j: (l, 0, j)),
        ],
        out_specs=pl.BlockSpec((None, m, tn), lambda l, j: (l, 0, j)),
        out_shape=jax.ShapeDtypeStruct((DEPTH, m, n), F32),
        compiler_params=_params(("arbitrary", "arbitrary")),
        name="ada",
    )(c_all, w_ada, b_ada.reshape(DEPTH, 1, n))


def _mod_spec(mod, chunk, tiles_per_mod):
    r = mod.shape[1]
    return pl.BlockSpec((None, r, D_MODEL), lambda i: (i // tiles_per_mod, 0, chunk))


def _in_kernel(x_ref, sh_ref, sc_ref, g_ref, w_ref, *out_refs):
    h = _rms(x_ref[...]) * g_ref[...]
    h = (h * (1.0 + sc_ref[...]) + sh_ref[...]).astype(BF16)
    off = 0
    for ref in out_refs:
        n = ref.shape[-1]
        ref[...] = _dot(h, w_ref[:, off:off + n])
        off += n


def _in_proj(x, mod, tiles_per_mod, norm_g, w_p):
    t = x.shape[0]
    tm = TOKEN_TILE
    return pl.pallas_call(
        _in_kernel,
        grid=(t // tm,),
        in_specs=[
            pl.BlockSpec((tm, D_MODEL), lambda i: (i, 0)),
            _mod_spec(mod, 0, tiles_per_mod),
            _mod_spec(mod, 1, tiles_per_mod),
            pl.BlockSpec((1, D_MODEL), lambda i: (0, 0)),
            pl.BlockSpec((D_MODEL, IN_PAD_W), lambda i: (0, 0)),
        ],
        out_specs=[pl.BlockSpec((tm, n), lambda i: (i, 0)) for n in IN_SPLITS],
        out_shape=[jax.ShapeDtypeStruct((t, n), F32) for n in IN_SPLITS],
        compiler_params=_params(("arbitrary",)),
        name="in_proj",
    )(x, mod, mod, norm_g, w_p)


def _inv_unit_lower(lms, nfac):
    c = lms[0].shape[0]
    r = lax.broadcasted_iota(jnp.int32, (c, c), 0)
    q = lax.broadcasted_iota(jnp.int32, (c, c), 1)
    eye = jnp.where(r == q, 1.0, 0.0)
    ts = [eye - lm for lm in lms]
    ps = list(lms)
    for _ in range(nfac - 1):
        pbs = [p.astype(BF16) for p in ps]
        ps = [_dot(pb, pb) for pb in pbs]
        ts = [t + _dot(t.astype(BF16), p.astype(BF16)) for t, p in zip(ts, ps)]
    splits = [(_split2(t), _split2(lm)) for t, lm in zip(ts, lms)]
    es = [(eye - t) - (_dot(lh, th) + (_dot(lh, tl) + _dot(ll, th)))
          for t, ((th, tl), (lh, ll)) in zip(ts, splits)]
    return [t + _dot(th, e.astype(BF16)) for t, ((th, _), _), e in zip(ts, splits, es)]


GDN_STACK = GDN_HEADS * GDN_CHUNK
GDN_PREP_CHUNKS = 4


def _gdn_prep_kernel(qkv_ref, hist_ref, ab_ref, cw_ref, alog_ref, dtb_ref,
                     u_ref, wq_ref, kdt_ref, qk_ref, egl_ref, xs_ref,
                     *, blk, lv, nfac, chunks_per_seq):
    c = GDN_CHUNK
    hc = GDN_STACK
    cb = GDN_PREP_CHUNKS
    hist = SUBLANES
    sh = blk.bit_length() - 1
    first = ((pl.program_id(0) * cb) % chunks_per_seq) == 0

    @pl.when(first)
    def _():
        xs_ref[0:hist, :] = jnp.zeros((hist, CONV_DIM), F32)

    @pl.when(jnp.logical_not(first))
    def _():
        xs_ref[0:hist, :] = hist_ref[...]

    xs_ref[hist:hist + cb * c, :] = qkv_ref[...]

    ri = lax.broadcasted_iota(jnp.int32, (hc, hc), 0)
    cj = lax.broadcasted_iota(jnp.int32, (hc, hc), 1)
    same = (ri >> sh) == (cj >> sh)
    incl = same & (ri >= cj)
    strict = same & (ri > cj)
    rt = lax.broadcasted_iota(jnp.int32, (c, c), 0)
    ct = lax.broadcasted_iota(jnp.int32, (c, c), 1)
    same_t = (rt >> sh) == (ct >> sh)
    cum_t = jnp.where(same_t & (rt >= ct), 1.0, 0.0)
    tot_t = jnp.where(same_t, 1.0, 0.0)
    if lv < blk:
        rvalid = (lax.broadcasted_iota(jnp.int32, (c, 1), 0) & (blk - 1)) >= (blk - lv)

    def stack_cols(a, lane0):
        return jnp.concatenate([a[:, lane0 + h:lane0 + h + 1] for h in range(GDN_HEADS)], axis=0)

    lms, rhs, qds, gtots = [], [], [], []
    for ci in range(cb):
        base = hist + ci * c

        def conv(c0, base=base):
            y = xs_ref[base - 3:base - 3 + c, c0:c0 + LANES] * cw_ref[0:1, c0:c0 + LANES]
            for i in range(1, CONV_W):
                y = y + xs_ref[base - 3 + i:base - 3 + i + c, c0:c0 + LANES] * cw_ref[i:i + 1, c0:c0 + LANES]
            y = _silu(y)
            return jnp.where(rvalid, y, 0.0) if lv < blk else y

        ab = ab_ref[ci * c:(ci + 1) * c, :]
        g_all = -jnp.exp(alog_ref[...]) * _softplus(ab + dtb_ref[...])
        beta_all = jax.nn.sigmoid(ab)
        if lv < blk:
            g_all = jnp.where(rvalid, g_all, 0.0)
            beta_all = jnp.where(rvalid, beta_all, 0.0)
        gcum = _dot_exact_lhs(cum_t, g_all)
        gtot = _dot_exact_lhs(tot_t, g_all)
        gc = stack_cols(gcum, 0)
        gl = stack_cols(gtot, 0)
        beta = stack_cols(beta_all, GDN_HEADS)
        gcb = jnp.broadcast_to(gc, (hc, LANES))
        gr = jnp.concatenate([gcb[0:LANES].T[0:1], gcb[LANES:].T[0:1]], axis=1)

        q = jnp.concatenate([conv(h * GDN_DK) for h in range(GDN_HEADS)], axis=0)
        k = jnp.concatenate([conv(GDN_QK_W + h * GDN_DK) for h in range(GDN_HEADS)], axis=0)
        v = jnp.concatenate([conv(2 * GDN_QK_W + h * GDN_DV) for h in range(GDN_HEADS)], axis=0)
        q = q * lax.rsqrt(jnp.sum(q * q, axis=-1, keepdims=True) + EPS) * (GDN_DK ** -0.5)
        k = k * lax.rsqrt(jnp.sum(k * k, axis=-1, keepdims=True) + EPS)

        decay = jnp.where(incl, jnp.exp(jnp.minimum(gc - gr, 0.0)), 0.0)
        kb = k * beta
        k16 = k.astype(BF16)
        lms.append(jnp.where(strict, _dot_nt(kb.astype(BF16), k16) * decay, 0.0))
        eg = jnp.exp(gc)
        rhs.append(_split2(jnp.concatenate([v * beta, kb * eg], axis=1)))
        qds.append(q * eg)
        gtots.append(gtot)
        kd = k * jnp.exp(gl - gc)
        qk_ref[ci] = (_dot_nt(q.astype(BF16), k16) * decay).astype(BF16)
        kdt_ref[ci] = kd.T.astype(BF16)

    tinvs = _inv_unit_lower(lms, nfac)
    for ci in range(cb):
        t16 = tinvs[ci].astype(BF16)
        rh, rl = rhs[ci]
        x = _dot(t16, rh) + _dot(t16, rl)
        w = x[:, GDN_DV:]
        for h in range(GDN_HEADS):
            sl = slice(h * c, (h + 1) * c)
            u_ref[ci * c:(ci + 1) * c, h * GDN_DV:(h + 1) * GDN_DV] = x[sl, :GDN_DV]
            wq_ref[ci, h] = jnp.concatenate([w[sl], qds[ci][sl]], axis=0).astype(BF16)
            egl = jnp.exp(jnp.broadcast_to(gtots[ci][:, h:h + 1], (c, LANES)))
            egl_ref[ci, h] = egl.reshape(c // SUBLANES, SUBLANES, LANES)[:, 0, :]


def _gdn_scan_kernel(u_ref, wq_ref, kdt_ref, qk_ref, egl_ref, s0_ref, o_ref, s_ref, *, blk, sb):
    c = GDN_CHUNK
    nb = c // blk
    sh = blk.bit_length() - 1
    n = pl.program_id(1)

    @pl.when(n == 0)
    def _():
        s_ref[...] = s0_ref[...]

    rblk2 = (lax.broadcasted_iota(jnp.int32, (2 * c, 1), 0) & (c - 1)) >> sh
    rblk_s = (lax.broadcasted_iota(jnp.int32, (GDN_STACK, 1), 0) & (c - 1)) >> sh
    rrs = []
    for b in range(sb):
        for h in range(GDN_HEADS):
            wq = wq_ref[b, h]
            rr = _dot(wq, s_ref[b, 0, h].astype(BF16))
            for j in range(1, nb):
                rr = jnp.where(rblk2 == j, _dot(wq, s_ref[b, j, h].astype(BF16)), rr)
            rrs.append(rr)
    vnews = []
    for b in range(sb):
        rb = rrs[b * GDN_HEADS:(b + 1) * GDN_HEADS]
        vnew = jnp.concatenate([u_ref[b, :, h * GDN_DV:(h + 1) * GDN_DV] - rb[h][:c]
                                for h in range(GDN_HEADS)], axis=0)
        vnews.append(vnew)
        o = jnp.concatenate([r[c:] for r in rb], axis=0) + _dot(qk_ref[b], vnew.astype(BF16))
        for h in range(GDN_HEADS):
            o_ref[b, :, h * GDN_DV:(h + 1) * GDN_DV] = o[h * c:(h + 1) * c]
    for b in range(sb):
        vnew = vnews[b]
        for j in range(nb):
            vj = (vnew if nb == 1 else jnp.where(rblk_s == j, vnew, 0.0)).astype(BF16)
            zero = jnp.zeros((c, GDN_DV), BF16)
            vbd = jnp.concatenate(
                [jnp.concatenate([vj[h * c:(h + 1) * c] if g == h else zero for g in range(GDN_HEADS)], axis=1)
                 for h in range(GDN_HEADS)], axis=0)
            upd = _dot(kdt_ref[b], vbd)
            for h in range(GDN_HEADS):
                s_ref[b, j, h] = (egl_ref[b, h, j:j + 1, :] * s_ref[b, j, h]
                                  + upd[:, h * GDN_DV:(h + 1) * GDN_DV])


def _gdn(qkv, ab, s0, conv_w, alog_row, dtb_row, *, blk, lv, chunks_per_seq, sb):
    rows = qkv.shape[0]
    c = GDN_CHUNK
    hc = GDN_STACK
    cb = GDN_PREP_CHUNKS
    nb = c // blk
    assert blk in (SUBLANES, c) and lv <= blk
    nfac = max(1, (lv - 1).bit_length())
    nch = rows // c
    nseq = nch // chunks_per_seq
    assert nseq * nb == s0.shape[0] and nch % cb == 0
    assert chunks_per_seq == 1 or chunks_per_seq % cb == 0
    hist_blocks = cb * c // SUBLANES
    prep = functools.partial(_gdn_prep_kernel, blk=blk, lv=lv, nfac=nfac, chunks_per_seq=chunks_per_seq)
    u, wq, kdt, qk, egl = pl.pallas_call(
        prep,
        grid=(nch // cb,),
        in_specs=[
            pl.BlockSpec((cb * c, CONV_DIM), lambda i: (i, 0)),
            pl.BlockSpec((SUBLANES, CONV_DIM), lambda i: (jnp.maximum(i * hist_blocks - 1, 0), 0)),
            pl.BlockSpec((cb * c, LANES), lambda i: (i, 0)),
            pl.BlockSpec((CONV_W, CONV_DIM), lambda i: (0, 0)),
            pl.BlockSpec((1, LANES), lambda i: (0, 0)),
            pl.BlockSpec((1, LANES), lambda i: (0, 0)),
        ],
        out_specs=[
            pl.BlockSpec((cb * c, GDN_V_W), lambda i: (i, 0)),
            pl.BlockSpec((cb, GDN_HEADS, 2 * c, GDN_DK), lambda i: (i, 0, 0, 0)),
            pl.BlockSpec((cb, GDN_DK, hc), lambda i: (i, 0, 0)),
            pl.BlockSpec((cb, hc, hc), lambda i: (i, 0, 0)),
            pl.BlockSpec((cb, GDN_HEADS, SUBLANES, LANES), lambda i: (i, 0, 0, 0)),
        ],
        out_shape=[
            jax.ShapeDtypeStruct((rows, GDN_V_W), F32),
            jax.ShapeDtypeStruct((nch, GDN_HEADS, 2 * c, GDN_DK), BF16),
            jax.ShapeDtypeStruct((nch, GDN_DK, hc), BF16),
            jax.ShapeDtypeStruct((nch, hc, hc), BF16),
            jax.ShapeDtypeStruct((nch, GDN_HEADS, SUBLANES, LANES), F32),
        ],
        scratch_shapes=[pltpu.VMEM((SUBLANES + cb * c, CONV_DIM), F32)],
        compiler_params=_params(("arbitrary",)),
        name="gdn_prep",
    )(qkv, qkv, ab, conv_w, alog_row, dtb_row)

    n = chunks_per_seq
    scan = functools.partial(_gdn_scan_kernel, blk=blk, sb=sb)
    seq5 = lambda a: a.reshape((nseq, n) + a.shape[1:])
    s0g = s0.reshape(nseq, nb, GDN_HEADS, GDN_DK, GDN_DV)
    o, s_new = pl.pallas_call(
        scan,
        grid=(nseq // sb, n),
        in_specs=[
            pl.BlockSpec((sb, c, GDN_V_W), lambda i, t: (i, t, 0)),
            pl.BlockSpec((sb, None, GDN_HEADS, 2 * c, GDN_DK), lambda i, t: (i, t, 0, 0, 0)),
            pl.BlockSpec((sb, None, GDN_DK, hc), lambda i, t: (i, t, 0, 0)),
            pl.BlockSpec((sb, None, hc, hc), lambda i, t: (i, t, 0, 0)),
            pl.BlockSpec((sb, None, GDN_HEADS, SUBLANES, LANES), lambda i, t: (i, t, 0, 0, 0)),
            pl.BlockSpec((sb, nb, GDN_HEADS, GDN_DK, GDN_DV), lambda i, t: (i, 0, 0, 0, 0)),
        ],
        out_specs=[
            pl.BlockSpec((sb, c, GDN_V_W), lambda i, t: (i, t, 0)),
            pl.BlockSpec((sb, nb, GDN_HEADS, GDN_DK, GDN_DV), lambda i, t: (i, 0, 0, 0, 0)),
        ],
        out_shape=[
            jax.ShapeDtypeStruct((nseq, n * c, GDN_V_W), F32),
            jax.ShapeDtypeStruct(s0g.shape, F32),
        ],
        compiler_params=_params(("arbitrary", "arbitrary")),
        name="gdn_scan",
    )(u.reshape(nseq, n * c, GDN_V_W), seq5(wq), seq5(kdt), seq5(qk), seq5(egl), s0g)
    return o.reshape(rows, GDN_V_W), s_new.reshape(s0.shape)


def _swa_kernel(q_ref, kc_ref, vc_ref, kp_ref, vp_ref, sink_ref, o_ref, *, qb, bb, prev_is_cache):
    w = WINDOW
    n = pl.program_id(1)
    rows = SWA_GROUP * qb
    qi = lax.broadcasted_iota(jnp.int32, (rows, 2 * w), 0) & (qb - 1)
    kj = lax.broadcasted_iota(jnp.int32, (rows, 2 * w), 1)
    ok = ((kj < w) & (kj >= qi)) | ((kj >= w) & (kj - w <= qi))
    if not prev_is_cache:
        ok = ok & ((kj >= w) | (n > 0))
    scale = SWA_HEAD_DIM ** -0.5

    def pad_rows(a):
        return a if qb == w else jnp.concatenate([a, jnp.zeros((w - qb, a.shape[1]), a.dtype)], axis=0)

    for b in range(bb):
        r0 = b * qb
        kc_all = pad_rows(kc_ref[r0:r0 + qb, :])
        vc_all = pad_rows(vc_ref[r0:r0 + qb, :])
        kp_all = kp_ref[b] if prev_is_cache else kp_ref[...]
        vp_all = vp_ref[b] if prev_is_cache else vp_ref[...]
        ss, sinks, vs = [], [], []
        ones = jnp.ones((2 * w, SWA_HEAD_DIM), BF16)
        for hk in range(SWA_KV_HEADS):
            c0 = hk * SWA_HEAD_DIM
            sl = slice(c0, c0 + SWA_HEAD_DIM)
            k_all = jnp.concatenate([kp_all[:, sl], kc_all[:, sl]], axis=0).astype(BF16)
            v_all = jnp.concatenate([vp_all[:, sl], vc_all[:, sl]], axis=0).astype(BF16)
            vs.append(jnp.concatenate([v_all, ones], axis=1))
            heads = [hk * SWA_GROUP + g for g in range(SWA_GROUP)]
            q = jnp.concatenate(
                [q_ref[r0:r0 + qb, hd * SWA_HEAD_DIM:(hd + 1) * SWA_HEAD_DIM] for hd in heads],
                axis=0).astype(BF16)
            sinks.append(jnp.concatenate(
                [jnp.broadcast_to(sink_ref[0:1, hd:hd + 1], (qb, 1)) for hd in heads], axis=0))
            ss.append(jnp.where(ok, _dot_nt(q, k_all) * scale, NEG_BIG))
        ms = [jnp.maximum(jnp.max(s, axis=-1, keepdims=True), sink) for s, sink in zip(ss, sinks)]
        accs = [_dot(jnp.exp(s - m).astype(BF16), v) for s, m, v in zip(ss, ms, vs)]
        for hk in range(SWA_KV_HEADS):
            acc = accs[hk]
            den = acc[:, SWA_HEAD_DIM:SWA_HEAD_DIM + 1] + jnp.exp(sinks[hk] - ms[hk])
            res = acc[:, :SWA_HEAD_DIM] / den
            for g in range(SWA_GROUP):
                hd = hk * SWA_GROUP + g
                o_ref[r0:r0 + qb, hd * SWA_HEAD_DIM:(hd + 1) * SWA_HEAD_DIM] = res[g * qb:(g + 1) * qb]


def _swa(q, kv, sink_row, *, qb, bb, blocks_per_seq, cache_k=None, cache_v=None):
    rows = q.shape[0]
    w = WINDOW
    prev_is_cache = cache_k is not None
    steps = rows // (qb * bb)
    nseq = steps // blocks_per_seq
    kern = functools.partial(_swa_kernel, qb=qb, bb=bb, prev_is_cache=prev_is_cache)
    cur = lambda i, n: (i * blocks_per_seq + n, 0)
    if prev_is_cache:
        kp_spec = pl.BlockSpec((bb, w, SWA_KV_W), lambda i, n: (i, 0, 0))
        vp_spec = kp_spec
        kp_arr, vp_arr = cache_k, cache_v
    else:
        assert qb == w and bb == 1
        kp_spec = pl.BlockSpec((w, SWA_KV_W), lambda i, n: (i * blocks_per_seq + jnp.maximum(n - 1, 0), 0))
        vp_spec = pl.BlockSpec((w, SWA_KV_W), lambda i, n: (i * blocks_per_seq + jnp.maximum(n - 1, 0), 1))
        kp_arr, vp_arr = kv, kv
    return pl.pallas_call(
        kern,
        grid=(nseq, blocks_per_seq),
        in_specs=[
            pl.BlockSpec((qb * bb, SWA_Q_W), cur),
            pl.BlockSpec((qb * bb, SWA_KV_W), cur),
            pl.BlockSpec((qb * bb, SWA_KV_W), lambda i, n: (i * blocks_per_seq + n, 1)),
            kp_spec,
            vp_spec,
            pl.BlockSpec((1, LANES), lambda i, n: (0, 0)),
        ],
        out_specs=pl.BlockSpec((qb * bb, SWA_Q_W), cur),
        out_shape=jax.ShapeDtypeStruct((rows, SWA_Q_W), F32),
        compiler_params=_params(("arbitrary", "arbitrary")),
        name="swa",
    )(q, kv, kv, kp_arr, vp_arr, sink_row)


def _mix_kernel(o_ref, z_ref, ob_ref, gates_ref, x_ref, gt_ref, gn_ref, wb_ref, wo_ref, y_ref):
    parts = []
    for h in range(GDN_HEADS):
        sl = slice(h * GDN_DV, (h + 1) * GDN_DV)
        parts.append(_rms(o_ref[:, sl]) * gn_ref[...] * _silu(z_ref[:, sl]))
    oa = jnp.concatenate(parts, axis=1).astype(BF16)
    pa = _dot(oa, wb_ref[0:GDN_V_W, :])
    pb = _dot(ob_ref[...].astype(BF16), wb_ref[GDN_V_W:, :])
    mixed = (jax.nn.sigmoid(gates_ref[:, 0:D_MODEL]) * pa
             + jax.nn.sigmoid(gates_ref[:, D_MODEL:]) * pb)
    y_ref[...] = x_ref[...] + gt_ref[...] * _dot(mixed.astype(BF16), wo_ref[...])


def _mix(o_raw, z, ob, gates, x, mod, tiles_per_mod, gn_row, wb, wo):
    t = x.shape[0]
    tm = TOKEN_TILE
    row = lambda n: pl.BlockSpec((tm, n), lambda i: (i, 0))
    full = lambda a: pl.BlockSpec(a.shape, lambda i: (0, 0))
    return pl.pallas_call(
        _mix_kernel,
        grid=(t // tm,),
        in_specs=[row(GDN_V_W), row(GDN_V_W), row(SWA_Q_W), row(2 * D_MODEL), row(D_MODEL),
                  _mod_spec(mod, 2, tiles_per_mod), full(gn_row), full(wb), full(wo)],
        out_specs=row(D_MODEL),
        out_shape=jax.ShapeDtypeStruct((t, D_MODEL), F32),
        compiler_params=_params(("arbitrary",)),
        name="mix",
    )(o_raw, z, ob, gates, x, mod, gn_row, wb, wo)


def _ffn_kernel(x_ref, sh_ref, sc_ref, gt_ref, g_ref, wg_ref, wu_ref, wd_ref, y_ref, h_ref, acc_ref):
    j = pl.program_id(1)

    @pl.when(j == 0)
    def _():
        h = _rms(x_ref[...]) * g_ref[...]
        h_ref[...] = (h * (1.0 + sc_ref[...]) + sh_ref[...]).astype(BF16)

    h = h_ref[...]
    a = (_silu(_dot(h, wg_ref[...])) * _dot(h, wu_ref[...])).astype(BF16)
    d = _dot(a, wd_ref[...])

    @pl.when(j == 0)
    def _():
        acc_ref[...] = d

    @pl.when(j > 0)
    def _():
        acc_ref[...] += d

    @pl.when(j == pl.num_programs(1) - 1)
    def _():
        y_ref[...] = x_ref[...] + gt_ref[...] * acc_ref[...]


def _ffn(x, mod, tiles_per_mod, norm_g, wg, wu, wd):
    t = x.shape[0]
    tm = 2 * TOKEN_TILE
    tpm = max(1, tiles_per_mod // 2)
    d_ff = wg.shape[1]
    tf = FFN_FF_TILE
    r = mod.shape[1]
    mspec = lambda chunk: pl.BlockSpec((None, r, D_MODEL), lambda i, j: (i // tpm, 0, chunk))
    return pl.pallas_call(
        _ffn_kernel,
        grid=(t // tm, d_ff // tf),
        in_specs=[
            pl.BlockSpec((tm, D_MODEL), lambda i, j: (i, 0)),
            mspec(3), mspec(4), mspec(5),
            pl.BlockSpec((1, D_MODEL), lambda i, j: (0, 0)),
            pl.BlockSpec((D_MODEL, tf), lambda i, j: (0, j)),
            pl.BlockSpec((D_MODEL, tf), lambda i, j: (0, j)),
            pl.BlockSpec((tf, D_MODEL), lambda i, j: (j, 0)),
        ],
        out_specs=pl.BlockSpec((tm, D_MODEL), lambda i, j: (i, 0)),
        out_shape=jax.ShapeDtypeStruct((t, D_MODEL), F32),
        scratch_shapes=[pltpu.VMEM((tm, D_MODEL), BF16), pltpu.VMEM((tm, D_MODEL), F32)],
        compiler_params=_params(("arbitrary", "arbitrary")),
        name="ffn",
    )(x, mod, mod, mod, norm_g, wg, wu, wd)


def _route_kernel(x_ref, sh_ref, sc_ref, g_ref, wr_ref, h_ref, idx_ref, wt_ref):
    h = _rms(x_ref[...]) * g_ref[...]
    h = h * (1.0 + sc_ref[...]) + sh_ref[...]
    h_ref[...] = h.astype(BF16)
    logits = _dot3(h, wr_ref[...])
    lane_i = lax.broadcasted_iota(jnp.int32, logits.shape, 1)
    lane = lane_i.astype(F32)
    logits = jnp.where(lane_i < N_EXPERTS, logits, NEG_BIG)
    m1 = jnp.max(logits, axis=-1, keepdims=True)
    i1 = jnp.min(jnp.where(logits == m1, lane, float(LANES)), axis=-1, keepdims=True)
    rest = jnp.where(lane == i1, NEG_BIG, logits)
    m2 = jnp.max(rest, axis=-1, keepdims=True)
    i2 = jnp.min(jnp.where(rest == m2, lane, float(LANES)), axis=-1, keepdims=True)
    e = jnp.exp(m2 - m1)
    den = 1.0 + e
    idx_ref[...] = jnp.where(lane_i == 0, i1, jnp.where(lane_i == 1, i2, 0.0)).astype(jnp.int32)
    wt_ref[...] = jnp.where(lane_i == 0, 1.0 / den, jnp.where(lane_i == 1, e / den, 0.0))


def _route(x, mod, tiles_per_mod, norm_g, wr_pad):
    t = x.shape[0]
    tm = TOKEN_TILE
    row = lambda n: pl.BlockSpec((tm, n), lambda i: (i, 0))
    return pl.pallas_call(
        _route_kernel,
        grid=(t // tm,),
        in_specs=[row(D_MODEL), _mod_spec(mod, 3, tiles_per_mod), _mod_spec(mod, 4, tiles_per_mod),
                  pl.BlockSpec((1, D_MODEL), lambda i: (0, 0)),
                  pl.BlockSpec((D_MODEL, LANES), lambda i: (0, 0))],
        out_specs=[row(D_MODEL), row(LANES), row(LANES)],
        out_shape=[jax.ShapeDtypeStruct((t, D_MODEL), BF16),
                   jax.ShapeDtypeStruct((t, LANES), jnp.int32),
                   jax.ShapeDtypeStruct((t, LANES), F32)],
        compiler_params=_params(("arbitrary",)),
        name="route",
    )(x, mod, mod, norm_g, wr_pad)


def _moe_kernel(te_ref, nu_ref, x_ref, wg_ref, wu_ref, wd_ref, o_ref):
    i = pl.program_id(0)
    j = pl.program_id(1)
    used = i < nu_ref[0]

    @pl.when(used)
    def _():
        x = x_ref[...]
        a = (_silu(_dot(x, wg_ref[...].astype(BF16))) * _dot(x, wu_ref[...].astype(BF16))).astype(BF16)
        d = _dot(a, wd_ref[...].astype(BF16))

        @pl.when(j == 0)
        def _():
            o_ref[...] = d

        @pl.when(j > 0)
        def _():
            o_ref[...] += d

    @pl.when(jnp.logical_not(used) & (j == 0))
    def _():
        o_ref[...] = jnp.zeros_like(o_ref)


def _moe(xs, tile_expert, n_used, wg, wu, wd):
    p = xs.shape[0]
    tm = MOE_TILE
    tf = MOE_FF_TILE
    d_ff = wg.shape[2]
    nj = d_ff // tf

    def jj(i, j, nu):
        return jnp.where(i < nu[0], j, nj - 1)

    grid_spec = pltpu.PrefetchScalarGridSpec(
        num_scalar_prefetch=2,
        grid=(p // tm, nj),
        in_specs=[
            pl.BlockSpec((tm, D_MODEL), lambda i, j, te, nu: (i, 0)),
            pl.BlockSpec((None, D_MODEL, tf), lambda i, j, te, nu: (te[i], 0, jj(i, j, nu))),
            pl.BlockSpec((None, D_MODEL, tf), lambda i, j, te, nu: (te[i], 0, jj(i, j, nu))),
            pl.BlockSpec((None, tf, D_MODEL), lambda i, j, te, nu: (te[i], jj(i, j, nu), 0)),
        ],
        out_specs=pl.BlockSpec((tm, D_MODEL), lambda i, j, te, nu: (i, 0)),
    )
    return pl.pallas_call(
        _moe_kernel,
        grid_spec=grid_spec,
        out_shape=jax.ShapeDtypeStruct((p, D_MODEL), F32),
        compiler_params=_params(("arbitrary", "arbitrary")),
        name="moe",
    )(tile_expert, n_used, xs, wg, wu, wd)


def _dispatch(top_i):
    t = top_i.shape[0]
    tm = MOE_TILE
    n_assign = t * TOP_K
    n_tiles = -(-n_assign // tm) + N_EXPERTS
    e_flat = top_i.reshape(-1)
    onehot = (e_flat[:, None] == jnp.arange(N_EXPERTS, dtype=jnp.int32)[None, :]).astype(jnp.int32)
    csum = jnp.cumsum(onehot, axis=0)
    counts = csum[-1]
    tiles_e = (counts + tm - 1) // tm
    tile_end = jnp.cumsum(tiles_e)
    tile_start = tile_end - tiles_e
    pos = jnp.sum(onehot * (csum - 1 + tile_start[None, :] * tm), axis=1)
    n_used = tile_end[-1]
    tile_ids = jnp.arange(n_tiles, dtype=jnp.int32)
    tile_expert = jnp.sum((tile_ids[:, None] >= tile_end[None, :]).astype(jnp.int32), axis=1)
    last_e = jnp.max(jnp.where(tiles_e > 0, jnp.arange(N_EXPERTS, dtype=jnp.int32), 0))
    tile_expert = jnp.where(tile_ids < n_used, tile_expert, last_e).astype(jnp.int32)
    p = n_tiles * tm
    tok = jnp.arange(n_assign, dtype=jnp.int32) // TOP_K
    row_token = jnp.zeros((p,), jnp.int32).at[pos].set(tok)
    return pos.reshape(t, TOP_K), row_token, tile_expert, n_used.reshape(1).astype(jnp.int32)


def _combine_kernel(x_ref, ya_ref, yb_ref, wt_ref, gt_ref, g_ref, y_ref):
    f = wt_ref[:, 0:1] * ya_ref[...] + wt_ref[:, 1:2] * yb_ref[...]
    x = x_ref[...] + gt_ref[...] * f
    y_ref[...] = _rms(x) * g_ref[...]


def _combine(x, ya, yb, wt, mod, tiles_per_mod, norm_g):
    t = x.shape[0]
    tm = TOKEN_TILE
    row = pl.BlockSpec((tm, D_MODEL), lambda i: (i, 0))
    return pl.pallas_call(
        _combine_kernel,
        grid=(t // tm,),
        in_specs=[row, row, row, pl.BlockSpec((tm, LANES), lambda i: (i, 0)),
                  _mod_spec(mod, 5, tiles_per_mod), pl.BlockSpec((1, D_MODEL), lambda i: (0, 0))],
        out_specs=row,
        out_shape=jax.ShapeDtypeStruct((t, D_MODEL), F32),
        compiler_params=_params(("arbitrary",)),
        name="combine",
    )(x, ya, yb, wt, mod, norm_g)


def _lane_row(vals):
    return jnp.zeros((1, LANES), F32).at[0, :vals.shape[0]].set(vals.astype(F32))


def _permute_w_in(w):
    c = [CONV_DIM, CONV_DIM + GDN_V_W]
    a0 = c[1]
    q0 = a0 + 2 * GDN_HEADS
    k0 = q0 + SWA_Q_W
    g0 = k0 + 2 * SWA_KV_W
    cols = [w[:, :c[0]], w[:, c[0]:c[1]], w[:, q0:k0], w[:, k0:g0], w[:, g0:],
            w[:, a0:q0], jnp.zeros((D_MODEL, LANES - 2 * GDN_HEADS), w.dtype)]
    return jnp.concatenate(cols, axis=1).astype(BF16)


def kernel(x_prompt, x_sample, c_prompt, c_sample, state_gdn, state_conv, cache_win_k, cache_win_v, w_in, conv_w, a_log, dt_bias, gdn_norm, attn_sinks, w_branch, w_out, w_ada, b_ada, norm_mix, norm_ffn, w_gate_dense, w_up_dense, w_down_dense, w_router, w_gate_moe, w_up_moe, w_down_moe, final_norm):
    bp, lp, _ = x_prompt.shape
    bs, ls, _ = x_sample.shape
    tp, ts = bp * lp, bs * ls
    tm = TOKEN_TILE
    pad_rows = SUBLANES - ls

    m_pad = -(-(bp + bs) // SUBLANES) * SUBLANES
    c_all = jnp.concatenate([c_prompt, c_sample, jnp.zeros((m_pad - bp - bs, D_MODEL), F32)], axis=0)
    ada = _ada(c_all, w_ada, b_ada)

    xp = x_prompt.reshape(tp, D_MODEL)
    xs = x_sample.reshape(ts, D_MODEL)
    outs = {k: [] for k in ("gdn_p", "conv_p", "k_p", "v_p", "gdn_s", "conv_s", "k_s", "v_s")}
    for l in range(DEPTH):
        mod_p = ada[l, :bp].reshape(bp, 1, 6 * D_MODEL)
        mod_s = jnp.repeat(ada[l, bp:bp + bs], ls, axis=0).reshape(ts // tm, tm, 6 * D_MODEL)
        tpm_p = lp // tm
        w_p = _permute_w_in(w_in[l])
        nmix = norm_mix[l].reshape(1, D_MODEL)
        nffn = norm_ffn[l].reshape(1, D_MODEL)
        alog_row, dtb_row = _lane_row(a_log[l]), _lane_row(dt_bias[l])
        sink_row = _lane_row(attn_sinks[l])
        gn_row = gdn_norm[l].reshape(1, GDN_DV)
        wb = w_branch[l].astype(BF16)
        wo = w_out[l].astype(BF16)

        qkv_p, z_p, qb_p, kv_p, gates_p, ab_p = _in_proj(xp, mod_p, tpm_p, nmix, w_p)
        s0_p = jnp.zeros((bp, GDN_HEADS, GDN_DK, GDN_DV), F32)
        o_p, s_p = _gdn(qkv_p, ab_p, s0_p, conv_w[l], alog_row, dtb_row,
                        blk=GDN_CHUNK, lv=GDN_CHUNK, chunks_per_seq=lp // GDN_CHUNK, sb=bp)
        ob_p = _swa(qb_p, kv_p, sink_row, qb=WINDOW, bb=1, blocks_per_seq=lp // WINDOW)
        xp = _mix(o_p, z_p, ob_p, gates_p, xp, mod_p, tpm_p, gn_row, wb, wo)
        outs["gdn_p"].append(s_p)
        outs["conv_p"].append(qkv_p.reshape(bp, lp, CONV_DIM)[:, lp - (CONV_W - 1):])
        kv3 = kv_p.reshape(bp, lp, 2 * SWA_KV_W)[:, lp - WINDOW:]
        outs["k_p"].append(kv3[..., :SWA_KV_W].reshape(bp, WINDOW, SWA_KV_HEADS, SWA_HEAD_DIM))
        outs["v_p"].append(kv3[..., SWA_KV_W:].reshape(bp, WINDOW, SWA_KV_HEADS, SWA_HEAD_DIM))

        qkv_s, z_s, qb_s, kv_s, gates_s, ab_s = _in_proj(xs, mod_s, 1, nmix, w_p)
        qkv_s3 = qkv_s.reshape(bs, ls, CONV_DIM)
        hist = jnp.concatenate([jnp.zeros((bs, SUBLANES - ls - (CONV_W - 1), CONV_DIM), F32),
                                state_conv[l], qkv_s3], axis=1)
        ab_blk = jnp.concatenate([jnp.zeros((bs, pad_rows, LANES), F32), ab_s.reshape(bs, ls, LANES)], axis=1)
        o_s, s_s = _gdn(hist.reshape(bs * SUBLANES, CONV_DIM), ab_blk.reshape(bs * SUBLANES, LANES),
                        state_gdn[l], conv_w[l], alog_row, dtb_row,
                        blk=SUBLANES, lv=ls, chunks_per_seq=1, sb=2)
        o_s = o_s.reshape(bs, SUBLANES, GDN_V_W)[:, pad_rows:].reshape(ts, GDN_V_W)
        pad_after = lambda a: jnp.concatenate(
            [a.reshape(bs, ls, -1), jnp.zeros((bs, pad_rows, a.shape[-1]), F32)], axis=1).reshape(bs * SUBLANES, -1)
        ck = cache_win_k[l].reshape(bs, WINDOW, SWA_KV_W)
        cv = cache_win_v[l].reshape(bs, WINDOW, SWA_KV_W)
        ob_s = _swa(pad_after(qb_s), pad_after(kv_s), sink_row, qb=SUBLANES, bb=8, blocks_per_seq=1,
                    cache_k=ck, cache_v=cv)
        ob_s = ob_s.reshape(bs, SUBLANES, SWA_Q_W)[:, :ls].reshape(ts, SWA_Q_W)
        xs = _mix(o_s, z_s, ob_s, gates_s, xs, mod_s, 1, gn_row, wb, wo)
        outs["gdn_s"].append(s_s)
        outs["conv_s"].append(jnp.concatenate([state_conv[l], qkv_s3], axis=1)[:, -(CONV_W - 1):])
        kv_s3 = kv_s.reshape(bs, ls, 2 * SWA_KV_W)
        knew = jnp.concatenate([ck, kv_s3[..., :SWA_KV_W]], axis=1)[:, -WINDOW:]
        vnew = jnp.concatenate([cv, kv_s3[..., SWA_KV_W:]], axis=1)[:, -WINDOW:]
        outs["k_s"].append(knew.reshape(bs, WINDOW, SWA_KV_HEADS, SWA_HEAD_DIM))
        outs["v_s"].append(vnew.reshape(bs, WINDOW, SWA_KV_HEADS, SWA_HEAD_DIM))

        i = l // 2
        if l % 2 == 0:
            wg, wu, wd = (w_gate_dense[i].astype(BF16), w_up_dense[i].astype(BF16),
                          w_down_dense[i].astype(BF16))
            xp = _ffn(xp, mod_p, tpm_p, nffn, wg, wu, wd)
            xs = _ffn(xs, mod_s.reshape(ts // (2 * tm), 2 * tm, 6 * D_MODEL), 1, nffn, wg, wu, wd)
        else:
            wr_pad = jnp.concatenate([w_router[i], jnp.zeros((D_MODEL, LANES - N_EXPERTS), F32)], axis=1)
            h_p, idx_p, wt_p = _route(xp, mod_p, tpm_p, nffn, wr_pad)
            h_s, idx_s, wt_s = _route(xs, mod_s, 1, nffn, wr_pad)
            h_all = jnp.concatenate([h_p, h_s], axis=0)
            top_i = jnp.concatenate([idx_p[:, :TOP_K], idx_s[:, :TOP_K]], axis=0)
            pos, row_token, tile_expert, n_used = _dispatch(top_i)
            h32 = lax.bitcast_convert_type(h_all.reshape(tp + ts, D_MODEL // 2, 2), jnp.uint32)
            xg = lax.bitcast_convert_type(jnp.take(h32, row_token, axis=0), BF16)
            y = _moe(xg.reshape(-1, D_MODEL), tile_expert, n_used,
                     w_gate_moe[i], w_up_moe[i], w_down_moe[i])
            ya = jnp.take(y, pos[:, 0], axis=0)
            yb = jnp.take(y, pos[:, 1], axis=0)
            fin = final_norm.reshape(1, D_MODEL)
            assert l == DEPTH - 1
            xp = _combine(xp, ya[:tp], yb[:tp], wt_p, mod_p, tpm_p, fin)
            xs = _combine(xs, ya[tp:], yb[tp:], wt_s, mod_s, 1, fin)

    st = lambda k: jnp.stack(outs[k])
    return (xp.reshape(bp, lp, D_MODEL), xs.reshape(bs, ls, D_MODEL),
            st("gdn_p"), st("conv_p"), st("k_p"), st("v_p"),
            st("gdn_s"), st("conv_s"), st("k_s"), st("v_s"))
```

```python
import functools

import jax
import jax.numpy as jnp
from jax import lax
from jax.experimental import pallas as pl
from jax.experimental.pallas import tpu as pltpu

F32 = jnp.float32
BF16 = jnp.bfloat16

D_MODEL = 1024
DEPTH = 2
GDN_HEADS = 4
GDN_DK = 128
GDN_DV = 128
CONV_W = 4
SWA_HEADS = 8
SWA_KV_HEADS = 2
SWA_HEAD_DIM = 64
SWA_GROUP = SWA_HEADS // SWA_KV_HEADS
WINDOW = 128
GDN_QK_W = GDN_HEADS * GDN_DK
GDN_V_W = GDN_HEADS * GDN_DV
CONV_DIM = 2 * GDN_QK_W + GDN_V_W
SWA_Q_W = SWA_HEADS * SWA_HEAD_DIM
SWA_KV_W = SWA_KV_HEADS * SWA_HEAD_DIM
N_EXPERTS = 8
TOP_K = 2
EPS = 1e-6

LANES = 128
SUBLANES = 8
VMEM_LIMIT = 56 * 1024 * 1024
NEG_BIG = -1e30

GDN_CHUNK = 64
TOKEN_TILE = 256
MOE_TILE = 1024
MOE_FF_TILE = 512
FFN_FF_TILE = 1408

IN_SPLITS = (CONV_DIM, GDN_V_W, SWA_Q_W, 2 * SWA_KV_W, 2 * D_MODEL, LANES)
IN_PAD_W = sum(IN_SPLITS)


def _silu(x):
    return x * jax.nn.sigmoid(x)


def _softplus(x):
    return jnp.maximum(x, 0.0) + jnp.log1p(jnp.exp(-jnp.abs(x)))


def _dot(a, b):
    return jnp.dot(a, b, preferred_element_type=F32)


def _dot_nt(a, b):
    return lax.dot_general(a, b, (((1,), (1,)), ((), ())), preferred_element_type=F32)


def _dot_tn(a, b):
    return lax.dot_general(a, b, (((0,), (0,)), ((), ())), preferred_element_type=F32)


def _split2(a):
    hi = a.astype(BF16)
    lo = (a - hi.astype(F32)).astype(BF16)
    return hi, lo


def _split3(a):
    hi = a.astype(BF16)
    r = a - hi.astype(F32)
    mid = r.astype(BF16)
    lo = (r - mid.astype(F32)).astype(BF16)
    return hi, mid, lo


def _dot3(a, b):
    ah, al = _split2(a)
    bh, bl = _split2(b)
    return _dot(ah, bh) + (_dot(ah, bl) + _dot(al, bh))


def _dot_exact_lhs(a01, b):
    a = a01.astype(BF16)
    bh, bm, bl = _split3(b)
    return _dot(a, bh) + (_dot(a, bm) + _dot(a, bl))


def _rms(x):
    return x * lax.rsqrt(jnp.mean(x * x, axis=-1, keepdims=True) + EPS)


def _params(sem):
    return pltpu.CompilerParams(dimension_semantics=sem, vmem_limit_bytes=VMEM_LIMIT)


def _ada_kernel(c_ref, w_ref, b_ref, o_ref):
    o_ref[...] = _dot3(_silu(c_ref[...]), w_ref[...]) + b_ref[...]


def _ada(c_all, w_ada, b_ada):
    m = c_all.shape[0]
    tn = 1536
    n = 6 * D_MODEL
    return pl.pallas_call(
        _ada_kernel,
        grid=(DEPTH, n // tn),
        in_specs=[
            pl.BlockSpec((m, D_MODEL), lambda l, j: (0, 0)),
            pl.BlockSpec((None, D_MODEL, tn), lambda l, j: (l, 0, j)),
            pl.BlockSpec((None, 1, tn), lambda l, j: (l, 0, j)),
        ],
        out_specs=pl.BlockSpec((None, m, tn), lambda l, j: (l, 0, j)),
        out_shape=jax.ShapeDtypeStruct((DEPTH, m, n), F32),
        compiler_params=_params(("arbitrary", "arbitrary")),
        name="ada",
    )(c_all, w_ada, b_ada.reshape(DEPTH, 1, n))


def _mod_spec(mod, chunk, tiles_per_mod):
    r = mod.shape[1]
    return pl.BlockSpec((None, r, D_MODEL), lambda i: (i // tiles_per_mod, 0, chunk))


def _in_kernel(x_ref, sh_ref, sc_ref, g_ref, w_ref, *out_refs):
    h = _rms(x_ref[...]) * g_ref[...]
    h = (h * (1.0 + sc_ref[...]) + sh_ref[...]).astype(BF16)
    off = 0
    for ref in out_refs:
        n = ref.shape[-1]
        ref[...] = _dot(h, w_ref[:, off:off + n])
        off += n


def _in_proj(x, mod, tiles_per_mod, norm_g, w_p):
    t = x.shape[0]
    tm = TOKEN_TILE
    return pl.pallas_call(
        _in_kernel,
        grid=(t // tm,),
        in_specs=[
            pl.BlockSpec((tm, D_MODEL), lambda i: (i, 0)),
            _mod_spec(mod, 0, tiles_per_mod),
            _mod_spec(mod, 1, tiles_per_mod),
            pl.BlockSpec((1, D_MODEL), lambda i: (0, 0)),
            pl.BlockSpec((D_MODEL, IN_PAD_W), lambda i: (0, 0)),
        ],
        out_specs=[pl.BlockSpec((tm, n), lambda i: (i, 0)) for n in IN_SPLITS],
        out_shape=[jax.ShapeDtypeStruct((t, n), F32) for n in IN_SPLITS],
        compiler_params=_params(("arbitrary",)),
        name="in_proj",
    )(x, mod, mod, norm_g, w_p)


def _inv_unit_lower(lms, nfac):
    c = lms[0].shape[0]
    r = lax.broadcasted_iota(jnp.int32, (c, c), 0)
    q = lax.broadcasted_iota(jnp.int32, (c, c), 1)
    eye = jnp.where(r == q, 1.0, 0.0)
    ts = [eye - lm for lm in lms]
    ps = list(lms)
    for _ in range(nfac - 1):
        pbs = [p.astype(BF16) for p in ps]
        ps = [_dot(pb, pb) for pb in pbs]
        ts = [t + _dot(t.astype(BF16), p.astype(BF16)) for t, p in zip(ts, ps)]
    splits = [(_split2(t), _split2(lm)) for t, lm in zip(ts, lms)]
    es = [(eye - t) - (_dot(lh, th) + (_dot(lh, tl) + _dot(ll, th)))
          for t, ((th, tl), (lh, ll)) in zip(ts, splits)]
    return [t + _dot(th, e.astype(BF16)) for t, ((th, _), _), e in zip(ts, splits, es)]


GDN_STACK = GDN_HEADS * GDN_CHUNK
GDN_PREP_CHUNKS = 4


def _gdn_prep_kernel(qkv_ref, hist_ref, ab_ref, cw_ref, alog_ref, dtb_ref,
                     u_ref, wq_ref, kdt_ref, qk_ref, egl_ref, xs_ref,
                     *, blk, lv, nfac, chunks_per_seq):
    c = GDN_CHUNK
    hc = GDN_STACK
    cb = GDN_PREP_CHUNKS
    hist = SUBLANES
    sh = blk.bit_length() - 1
    first = ((pl.program_id(0) * cb) % chunks_per_seq) == 0

    @pl.when(first)
    def _():
        xs_ref[0:hist, :] = jnp.zeros((hist, CONV_DIM), F32)

    @pl.when(jnp.logical_not(first))
    def _():
        xs_ref[0:hist, :] = hist_ref[...]

    xs_ref[hist:hist + cb * c, :] = qkv_ref[...]

    ri = lax.broadcasted_iota(jnp.int32, (hc, hc), 0)
    cj = lax.broadcasted_iota(jnp.int32, (hc, hc), 1)
    same = (ri >> sh) == (cj >> sh)
    incl = same & (ri >= cj)
    strict = same & (ri > cj)
    rt = lax.broadcasted_iota(jnp.int32, (c, c), 0)
    ct = lax.broadcasted_iota(jnp.int32, (c, c), 1)
    same_t = (rt >> sh) == (ct >> sh)
    cum_t = jnp.where(same_t & (rt >= ct), 1.0, 0.0)
    tot_t = jnp.where(same_t, 1.0, 0.0)
    if lv < blk:
        rvalid = (lax.broadcasted_iota(jnp.int32, (c, 1), 0) & (blk - 1)) >= (blk - lv)

    def stack_cols(a, lane0):
        return jnp.concatenate([a[:, lane0 + h:lane0 + h + 1] for h in range(GDN_HEADS)], axis=0)

    lms, rhs, qds, gtots = [], [], [], []
    for ci in range(cb):
        base = hist + ci * c

        def conv(c0, base=base):
            y = xs_ref[base - 3:base - 3 + c, c0:c0 + LANES] * cw_ref[0:1, c0:c0 + LANES]
            for i in range(1, CONV_W):
                y = y + xs_ref[base - 3 + i:base - 3 + i + c, c0:c0 + LANES] * cw_ref[i:i + 1, c0:c0 + LANES]
            y = _silu(y)
            return jnp.where(rvalid, y, 0.0) if lv < blk else y

        ab = ab_ref[ci * c:(ci + 1) * c, :]
        g_all = -jnp.exp(alog_ref[...]) * _softplus(ab + dtb_ref[...])
        beta_all = jax.nn.sigmoid(ab)
        if lv < blk:
            g_all = jnp.where(rvalid, g_all, 0.0)
            beta_all = jnp.where(rvalid, beta_all, 0.0)
        gcum = _dot_exact_lhs(cum_t, g_all)
        gtot = _dot_exact_lhs(tot_t, g_all)
        gc = stack_cols(gcum, 0)
        gl = stack_cols(gtot, 0)
        beta = stack_cols(beta_all, GDN_HEADS)
        gcb = jnp.broadcast_to(gc, (hc, LANES))
        gr = jnp.concatenate([gcb[0:LANES].T[0:1], gcb[LANES:].T[0:1]], axis=1)

        q = jnp.concatenate([conv(h * GDN_DK) for h in range(GDN_HEADS)], axis=0)
        k = jnp.concatenate([conv(GDN_QK_W + h * GDN_DK) for h in range(GDN_HEADS)], axis=0)
        v = jnp.concatenate([conv(2 * GDN_QK_W + h * GDN_DV) for h in range(GDN_HEADS)], axis=0)
        q = q * lax.rsqrt(jnp.sum(q * q, axis=-1, keepdims=True) + EPS) * (GDN_DK ** -0.5)
        k = k * lax.rsqrt(jnp.sum(k * k, axis=-1, keepdims=True) + EPS)

        decay = jnp.where(incl, jnp.exp(jnp.minimum(gc - gr, 0.0)), 0.0)
        kb = k * beta
        k16 = k.astype(BF16)
        lms.append(jnp.where(strict, _dot_nt(kb.astype(BF16), k16) * decay, 0.0))
        eg = jnp.exp(gc)
        rhs.append(_split2(jnp.concatenate([v * beta, kb * eg], axis=1)))
        qds.append(q * eg)
        gtots.append(gtot)
        kd = k * jnp.exp(gl - gc)
        qk_ref[ci] = (_dot_nt(q.astype(BF16), k16) * decay).astype(BF16)
        kdt_ref[ci] = kd.T.astype(BF16)

    tinvs = _inv_unit_lower(lms, nfac)
    for ci in range(cb):
        t16 = tinvs[ci].astype(BF16)
        rh, rl = rhs[ci]
        x = _dot(t16, rh) + _dot(t16, rl)
        w = x[:, GDN_DV:]
        for h in range(GDN_HEADS):
            sl = slice(h * c, (h + 1) * c)
            u_ref[ci * c:(ci + 1) * c, h * GDN_DV:(h + 1) * GDN_DV] = x[sl, :GDN_DV]
            wq_ref[ci, h] = jnp.concatenate([w[sl], qds[ci][sl]], axis=0).astype(BF16)
            egl = jnp.exp(jnp.broadcast_to(gtots[ci][:, h:h + 1], (c, LANES)))
            egl_ref[ci, h] = egl.reshape(c // SUBLANES, SUBLANES, LANES)[:, 0, :]


def _gdn_scan_kernel(u_ref, wq_ref, kdt_ref, qk_ref, egl_ref, s0_ref, o_ref, s_ref, *, blk, sb):
    c = GDN_CHUNK
    nb = c // blk
    sh = blk.bit_length() - 1
    n = pl.program_id(1)

    @pl.when(n == 0)
    def _():
        s_ref[...] = s0_ref[...]

    rblk2 = (lax.broadcasted_iota(jnp.int32, (2 * c, 1), 0) & (c - 1)) >> sh
    rblk_s = (lax.broadcasted_iota(jnp.int32, (GDN_STACK, 1), 0) & (c - 1)) >> sh
    rrs = []
    for b in range(sb):
        for h in range(GDN_HEADS):
            wq = wq_ref[b, h]
            rr = _dot(wq, s_ref[b, 0, h].astype(BF16))
            for j in range(1, nb):
                rr = jnp.where(rblk2 == j, _dot(wq, s_ref[b, j, h].astype(BF16)), rr)
            rrs.append(rr)
    vnews = []
    for b in range(sb):
        rb = rrs[b * GDN_HEADS:(b + 1) * GDN_HEADS]
        vnew = jnp.concatenate([u_ref[b, :, h * GDN_DV:(h + 1) * GDN_DV] - rb[h][:c]
                                for h in range(GDN_HEADS)], axis=0)
        vnews.append(vnew)
        o = jnp.concatenate([r[c:] for r in rb], axis=0) + _dot(qk_ref[b], vnew.astype(BF16))
        for h in range(GDN_HEADS):
            o_ref[b, :, h * GDN_DV:(h + 1) * GDN_DV] = o[h * c:(h + 1) * c]
    for b in range(sb):
        vnew = vnews[b]
        for j in range(nb):
            vj = (vnew if nb == 1 else jnp.where(rblk_s == j, vnew, 0.0)).astype(BF16)
            zero = jnp.zeros((c, GDN_DV), BF16)
            vbd = jnp.concatenate(
                [jnp.concatenate([vj[h * c:(h + 1) * c] if g == h else zero for g in range(GDN_HEADS)], axis=1)
                 for h in range(GDN_HEADS)], axis=0)
            upd = _dot(kdt_ref[b], vbd)
            for h in range(GDN_HEADS):
                s_ref[b, j, h] = (egl_ref[b, h, j:j + 1, :] * s_ref[b, j, h]
                                  + upd[:, h * GDN_DV:(h + 1) * GDN_DV])


def _gdn(qkv, ab, s0, conv_w, alog_row, dtb_row, *, blk, lv, chunks_per_seq, sb):
    rows = qkv.shape[0]
    c = GDN_CHUNK
    hc = GDN_STACK
    cb = GDN_PREP_CHUNKS
    nb = c // blk
    assert blk in (SUBLANES, c) and lv <= blk
    nfac = max(1, (lv - 1).bit_length())
    nch = rows // c
    nseq = nch // chunks_per_seq
    assert nseq * nb == s0.shape[0] and nch % cb == 0
    assert chunks_per_seq == 1 or chunks_per_seq % cb == 0
    hist_blocks = cb * c // SUBLANES
    prep = functools.partial(_gdn_prep_kernel, blk=blk, lv=lv, nfac=nfac, chunks_per_seq=chunks_per_seq)
    u, wq, kdt, qk, egl = pl.pallas_call(
        prep,
        grid=(nch // cb,),
        in_specs=[
            pl.BlockSpec((cb * c, CONV_DIM), lambda i: (i, 0)),
            pl.BlockSpec((SUBLANES, CONV_DIM), lambda i: (jnp.maximum(i * hist_blocks - 1, 0), 0)),
            pl.BlockSpec((cb * c, LANES), lambda i: (i, 0)),
            pl.BlockSpec((CONV_W, CONV_DIM), lambda i: (0, 0)),
            pl.BlockSpec((1, LANES), lambda i: (0, 0)),
            pl.BlockSpec((1, LANES), lambda i: (0, 0)),
        ],
        out_specs=[
            pl.BlockSpec((cb * c, GDN_V_W), lambda i: (i, 0)),
            pl.BlockSpec((cb, GDN_HEADS, 2 * c, GDN_DK), lambda i: (i, 0, 0, 0)),
            pl.BlockSpec((cb, GDN_DK, hc), lambda i: (i, 0, 0)),
            pl.BlockSpec((cb, hc, hc), lambda i: (i, 0, 0)),
            pl.BlockSpec((cb, GDN_HEADS, SUBLANES, LANES), lambda i: (i, 0, 0, 0)),
        ],
        out_shape=[
            jax.ShapeDtypeStruct((rows, GDN_V_W), F32),
            jax.ShapeDtypeStruct((nch, GDN_HEADS, 2 * c, GDN_DK), BF16),
            jax.ShapeDtypeStruct((nch, GDN_DK, hc), BF16),
            jax.ShapeDtypeStruct((nch, hc, hc), BF16),
            jax.ShapeDtypeStruct((nch, GDN_HEADS, SUBLANES, LANES), F32),
        ],
        scratch_shapes=[pltpu.VMEM((SUBLANES + cb * c, CONV_DIM), F32)],
        compiler_params=_params(("arbitrary",)),
        name="gdn_prep",
    )(qkv, qkv, ab, conv_w, alog_row, dtb_row)

    n = chunks_per_seq
    scan = functools.partial(_gdn_scan_kernel, blk=blk, sb=sb)
    seq5 = lambda a: a.reshape((nseq, n) + a.shape[1:])
    s0g = s0.reshape(nseq, nb, GDN_HEADS, GDN_DK, GDN_DV)
    o, s_new = pl.pallas_call(
        scan,
        grid=(nseq // sb, n),
        in_specs=[
            pl.BlockSpec((sb, c, GDN_V_W), lambda i, t: (i, t, 0)),
            pl.BlockSpec((sb, None, GDN_HEADS, 2 * c, GDN_DK), lambda i, t: (i, t, 0, 0, 0)),
            pl.BlockSpec((sb, None, GDN_DK, hc), lambda i, t: (i, t, 0, 0)),
            pl.BlockSpec((sb, None, hc, hc), lambda i, t: (i, t, 0, 0)),
            pl.BlockSpec((sb, None, GDN_HEADS, SUBLANES, LANES), lambda i, t: (i, t, 0, 0, 0)),
            pl.BlockSpec((sb, nb, GDN_HEADS, GDN_DK, GDN_DV), lambda i, t: (i, 0, 0, 0, 0)),
        ],
        out_specs=[
            pl.BlockSpec((sb, c, GDN_V_W), lambda i, t: (i, t, 0)),
            pl.BlockSpec((sb, nb, GDN_HEADS, GDN_DK, GDN_DV), lambda i, t: (i, 0, 0, 0, 0)),
        ],
        out_shape=[
            jax.ShapeDtypeStruct((nseq, n * c, GDN_V_W), F32),
            jax.ShapeDtypeStruct(s0g.shape, F32),
        ],
        compiler_params=_params(("arbitrary", "arbitrary")),
        name="gdn_scan",
    )(u.reshape(nseq, n * c, GDN_V_W), seq5(wq), seq5(kdt), seq5(qk), seq5(egl), s0g)
    return o.reshape(rows, GDN_V_W), s_new.reshape(s0.shape)


def _swa_kernel(q_ref, kc_ref, vc_ref, kp_ref, vp_ref, sink_ref, o_ref, *, qb, bb, prev_is_cache):
    w = WINDOW
    n = pl.program_id(1)
    rows = SWA_GROUP * qb
    qi = lax.broadcasted_iota(jnp.int32, (rows, 2 * w), 0) & (qb - 1)
    kj = lax.broadcasted_iota(jnp.int32, (rows, 2 * w), 1)
    ok = ((kj < w) & (kj >= qi)) | ((kj >= w) & (kj - w <= qi))
    if not prev_is_cache:
        ok = ok & ((kj >= w) | (n > 0))
    scale = SWA_HEAD_DIM ** -0.5

    def pad_rows(a):
        return a if qb == w else jnp.concatenate([a, jnp.zeros((w - qb, a.shape[1]), a.dtype)], axis=0)

    for b in range(bb):
        r0 = b * qb
        kc_all = pad_rows(kc_ref[r0:r0 + qb, :])
        vc_all = pad_rows(vc_ref[r0:r0 + qb, :])
        kp_all = kp_ref[b] if prev_is_cache else kp_ref[...]
        vp_all = vp_ref[b] if prev_is_cache else vp_ref[...]
        ss, sinks, vs = [], [], []
        ones = jnp.ones((2 * w, SWA_HEAD_DIM), BF16)
        for hk in range(SWA_KV_HEADS):
            c0 = hk * SWA_HEAD_DIM
            sl = slice(c0, c0 + SWA_HEAD_DIM)
            k_all = jnp.concatenate([kp_all[:, sl], kc_all[:, sl]], axis=0).astype(BF16)
            v_all = jnp.concatenate([vp_all[:, sl], vc_all[:, sl]], axis=0).astype(BF16)
            vs.append(jnp.concatenate([v_all, ones], axis=1))
            heads = [hk * SWA_GROUP + g for g in range(SWA_GROUP)]
            q = jnp.concatenate(
                [q_ref[r0:r0 + qb, hd * SWA_HEAD_DIM:(hd + 1) * SWA_HEAD_DIM] for hd in heads],
                axis=0).astype(BF16)
            sinks.append(jnp.concatenate(
                [jnp.broadcast_to(sink_ref[0:1, hd:hd + 1], (qb, 1)) for hd in heads], axis=0))
            ss.append(jnp.where(ok, _dot_nt(q, k_all) * scale, NEG_BIG))
        ms = [jnp.maximum(jnp.max(s, axis=-1, keepdims=True), sink) for s, sink in zip(ss, sinks)]
        accs = [_dot(jnp.exp(s - m).astype(BF16), v) for s, m, v in zip(ss, ms, vs)]
        for hk in range(SWA_KV_HEADS):
            acc = accs[hk]
            den = acc[:, SWA_HEAD_DIM:SWA_HEAD_DIM + 1] + jnp.exp(sinks[hk] - ms[hk])
            res = acc[:, :SWA_HEAD_DIM] / den
            for g in range(SWA_GROUP):
                hd = hk * SWA_GROUP + g
                o_ref[r0:r0 + qb, hd * SWA_HEAD_DIM:(hd + 1) * SWA_HEAD_DIM] = res[g * qb:(g + 1) * qb]


def _swa(q, kv, sink_row, *, qb, bb, blocks_per_seq, cache_k=None, cache_v=None):
    rows = q.shape[0]
    w = WINDOW
    prev_is_cache = cache_k is not None
    steps = rows // (qb * bb)
    nseq = steps // blocks_per_seq
    kern = functools.partial(_swa_kernel, qb=qb, bb=bb, prev_is_cache=prev_is_cache)
    cur = lambda i, n: (i * blocks_per_seq + n, 0)
    if prev_is_cache:
        kp_spec = pl.BlockSpec((bb, w, SWA_KV_W), lambda i, n: (i, 0, 0))
        vp_spec = kp_spec
        kp_arr, vp_arr = cache_k, cache_v
    else:
        assert qb == w and bb == 1
        kp_spec = pl.BlockSpec((w, SWA_KV_W), lambda i, n: (i * blocks_per_seq + jnp.maximum(n - 1, 0), 0))
        vp_spec = pl.BlockSpec((w, SWA_KV_W), lambda i, n: (i * blocks_per_seq + jnp.maximum(n - 1, 0), 1))
        kp_arr, vp_arr = kv, kv
    return pl.pallas_call(
        kern,
        grid=(nseq, blocks_per_seq),
        in_specs=[
            pl.BlockSpec((qb * bb, SWA_Q_W), cur),
            pl.BlockSpec((qb * bb, SWA_KV_W), cur),
            pl.BlockSpec((qb * bb, SWA_KV_W), lambda i, n: (i * blocks_per_seq + n, 1)),
            kp_spec,
            vp_spec,
            pl.BlockSpec((1, LANES), lambda i, n: (0, 0)),
        ],
        out_specs=pl.BlockSpec((qb * bb, SWA_Q_W), cur),
        out_shape=jax.ShapeDtypeStruct((rows, SWA_Q_W), F32),
        compiler_params=_params(("arbitrary", "arbitrary")),
        name="swa",
    )(q, kv, kv, kp_arr, vp_arr, sink_row)


def _mix_kernel(o_ref, z_ref, ob_ref, gates_ref, x_ref, gt_ref, gn_ref, wb_ref, wo_ref, y_ref):
    parts = []
    for h in range(GDN_HEADS):
        sl = slice(h * GDN_DV, (h + 1) * GDN_DV)
        parts.append(_rms(o_ref[:, sl]) * gn_ref[...] * _silu(z_ref[:, sl]))
    oa = jnp.concatenate(parts, axis=1).astype(BF16)
    pa = _dot(oa, wb_ref[0:GDN_V_W, :])
    pb = _dot(ob_ref[...].astype(BF16), wb_ref[GDN_V_W:, :])
    mixed = (jax.nn.sigmoid(gates_ref[:, 0:D_MODEL]) * pa
             + jax.nn.sigmoid(gates_ref[:, D_MODEL:]) * pb)
    y_ref[...] = x_ref[...] + gt_ref[...] * _dot(mixed.astype(BF16), wo_ref[...])


def _mix(o_raw, z, ob, gates, x, mod, tiles_per_mod, gn_row, wb, wo):
    t = x.shape[0]
    tm = TOKEN_TILE
    row = lambda n: pl.BlockSpec((tm, n), lambda i: (i, 0))
    full = lambda a: pl.BlockSpec(a.shape, lambda i: (0, 0))
    return pl.pallas_call(
        _mix_kernel,
        grid=(t // tm,),
        in_specs=[row(GDN_V_W), row(GDN_V_W), row(SWA_Q_W), row(2 * D_MODEL), row(D_MODEL),
                  _mod_spec(mod, 2, tiles_per_mod), full(gn_row), full(wb), full(wo)],
        out_specs=row(D_MODEL),
        out_shape=jax.ShapeDtypeStruct((t, D_MODEL), F32),
        compiler_params=_params(("arbitrary",)),
        name="mix",
    )(o_raw, z, ob, gates, x, mod, gn_row, wb, wo)


def _ffn_kernel(x_ref, sh_ref, sc_ref, gt_ref, g_ref, wg_ref, wu_ref, wd_ref, y_ref, h_ref, acc_ref):
    j = pl.program_id(1)

    @pl.when(j == 0)
    def _():
        h = _rms(x_ref[...]) * g_ref[...]
        h_ref[...] = (h * (1.0 + sc_ref[...]) + sh_ref[...]).astype(BF16)

    h = h_ref[...]
    a = (_silu(_dot(h, wg_ref[...])) * _dot(h, wu_ref[...])).astype(BF16)
    d = _dot(a, wd_ref[...])

    @pl.when(j == 0)
    def _():
        acc_ref[...] = d

    @pl.when(j > 0)
    def _():
        acc_ref[...] += d

    @pl.when(j == pl.num_programs(1) - 1)
    def _():
        y_ref[...] = x_ref[...] + gt_ref[...] * acc_ref[...]


def _ffn(x, mod, tiles_per_mod, norm_g, wg, wu, wd):
    t = x.shape[0]
    tm = 2 * TOKEN_TILE
    tpm = max(1, tiles_per_mod // 2)
    d_ff = wg.shape[1]
    tf = FFN_FF_TILE
    r = mod.shape[1]
    mspec = lambda chunk: pl.BlockSpec((None, r, D_MODEL), lambda i, j: (i // tpm, 0, chunk))
    return pl.pallas_call(
        _ffn_kernel,
        grid=(t // tm, d_ff // tf),
        in_specs=[
            pl.BlockSpec((tm, D_MODEL), lambda i, j: (i, 0)),
            mspec(3), mspec(4), mspec(5),
            pl.BlockSpec((1, D_MODEL), lambda i, j: (0, 0)),
            pl.BlockSpec((D_MODEL, tf), lambda i, j: (0, j)),
            pl.BlockSpec((D_MODEL, tf), lambda i, j: (0, j)),
            pl.BlockSpec((tf, D_MODEL), lambda i, j: (j, 0)),
        ],
        out_specs=pl.BlockSpec((tm, D_MODEL), lambda i, j: (i, 0)),
        out_shape=jax.ShapeDtypeStruct((t, D_MODEL), F32),
        scratch_shapes=[pltpu.VMEM((tm, D_MODEL), BF16), pltpu.VMEM((tm, D_MODEL), F32)],
        compiler_params=_params(("arbitrary", "arbitrary")),
        name="ffn",
    )(x, mod, mod, mod, norm_g, wg, wu, wd)


def _route_kernel(x_ref, sh_ref, sc_ref, g_ref, wr_ref, base_ref,
                  h_ref, idx_ref, wt_ref, rank_ref, cnt_ref, run_ref):
    @pl.when(pl.program_id(0) == 0)
    def _():
        run_ref[...] = base_ref[...]

    h = _rms(x_ref[...]) * g_ref[...]
    h = h * (1.0 + sc_ref[...]) + sh_ref[...]
    h_ref[...] = h
    logits = _dot3(h, wr_ref[...])
    lane_i = lax.broadcasted_iota(jnp.int32, logits.shape, 1)
    lane = lane_i.astype(F32)
    logits = jnp.where(lane_i < N_EXPERTS, logits, NEG_BIG)
    m1 = jnp.max(logits, axis=-1, keepdims=True)
    i1 = jnp.min(jnp.where(logits == m1, lane, float(LANES)), axis=-1, keepdims=True)
    rest = jnp.where(lane == i1, NEG_BIG, logits)
    m2 = jnp.max(rest, axis=-1, keepdims=True)
    i2 = jnp.min(jnp.where(rest == m2, lane, float(LANES)), axis=-1, keepdims=True)
    e = jnp.exp(m2 - m1)
    den = 1.0 + e
    idx_ref[...] = jnp.where(lane_i == 0, i1, jnp.where(lane_i == 1, i2, 0.0)).astype(jnp.int32)
    wt_ref[...] = jnp.where(lane_i == 0, 1.0 / den, jnp.where(lane_i == 1, e / den, 0.0))
    tm = logits.shape[0]
    oh1 = lane == i1
    oh2 = lane == i2
    cmat = jnp.where(oh1 | oh2, 1.0, 0.0)
    rt = lax.broadcasted_iota(jnp.int32, (tm, tm), 0)
    ct = lax.broadcasted_iota(jnp.int32, (tm, tm), 1)
    before = _dot(jnp.where(rt > ct, 1.0, 0.0).astype(BF16), cmat.astype(BF16)) + run_ref[0:1, :]
    r1 = jnp.sum(jnp.where(oh1, before, 0.0), axis=-1, keepdims=True)
    r2 = jnp.sum(jnp.where(oh2, before, 0.0), axis=-1, keepdims=True)
    rank_ref[...] = jnp.where(lane_i == 0, r1, jnp.where(lane_i == 1, r2, 0.0)).astype(jnp.int32)
    run_ref[0:1, :] = run_ref[0:1, :] + jnp.sum(cmat, axis=0, keepdims=True)
    cnt_ref[...] = run_ref[...]


def _route(x, mod, tiles_per_mod, norm_g, wr_pad, base):
    t = x.shape[0]
    tm = TOKEN_TILE
    assert tm <= 256
    row = lambda n: pl.BlockSpec((tm, n), lambda i: (i, 0))
    cst = lambda a, b: pl.BlockSpec((a, b), lambda i: (0, 0))
    return pl.pallas_call(
        _route_kernel,
        grid=(t // tm,),
        in_specs=[row(D_MODEL), _mod_spec(mod, 3, tiles_per_mod), _mod_spec(mod, 4, tiles_per_mod),
                  cst(1, D_MODEL), cst(D_MODEL, LANES), cst(SUBLANES, LANES)],
        out_specs=[row(D_MODEL), row(LANES), row(LANES), row(LANES), cst(SUBLANES, LANES)],
        out_shape=[jax.ShapeDtypeStruct((t, D_MODEL), F32),
                   jax.ShapeDtypeStruct((t, LANES), jnp.int32),
                   jax.ShapeDtypeStruct((t, LANES), F32),
                   jax.ShapeDtypeStruct((t, LANES), jnp.int32),
                   jax.ShapeDtypeStruct((SUBLANES, LANES), F32)],
        scratch_shapes=[pltpu.VMEM((SUBLANES, LANES), F32)],
        compiler_params=_params(("arbitrary",)),
        name="route",
    )(x, mod, mod, norm_g, wr_pad, base)


MOVE_ROWS = 1024


def _move_rows_kernel(sidx_ref, didx_ref, src_ref, *rest):
    dst_ref, sem = rest[-2], rest[-1]

    def copy(r):
        return pltpu.make_async_copy(src_ref.at[pl.ds(sidx_ref[r], 1)],
                                     dst_ref.at[pl.ds(didx_ref[r], 1)], sem)

    def start(r, carry):
        copy(r).start()
        return carry

    def wait(r, carry):
        copy(r).wait()
        return carry

    lax.fori_loop(0, MOVE_ROWS, start, 0, unroll=8)
    lax.fori_loop(0, MOVE_ROWS, wait, 0, unroll=8)


def _move_rows(src, sidx, didx, dst_init=None, out_rows=None):
    n = sidx.shape[0]
    assert n % MOVE_ROWS == 0 and didx.shape[0] == n
    idx_spec = pl.BlockSpec((MOVE_ROWS,), lambda i: (i,), memory_space=pltpu.SMEM)
    any_spec = pl.BlockSpec(memory_space=pl.ANY)
    ins = [sidx, didx, src]
    in_specs = [idx_spec, idx_spec, any_spec]
    aliases = {}
    if dst_init is not None:
        ins.append(dst_init)
        in_specs.append(any_spec)
        aliases = {3: 0}
        out_rows = dst_init.shape[0]
    return pl.pallas_call(
        _move_rows_kernel,
        grid=(n // MOVE_ROWS,),
        in_specs=in_specs,
        out_specs=any_spec,
        out_shape=jax.ShapeDtypeStruct((out_rows, src.shape[1]), src.dtype),
        scratch_shapes=[pltpu.SemaphoreType.DMA(())],
        input_output_aliases=aliases,
        compiler_params=_params(("arbitrary",)),
        name="move_rows",
    )(*ins)


def _moe_kernel(te_ref, nu_ref, x_ref, wg_ref, wu_ref, wd_ref, o_ref, xb_ref):
    i = pl.program_id(0)
    j = pl.program_id(1)
    used = i < nu_ref[0]

    @pl.when(used & (j == 0))
    def _():
        xb_ref[...] = x_ref[...].astype(BF16)

    @pl.when(used)
    def _():
        x = xb_ref[...]
        a = (_silu(_dot(x, wg_ref[...].astype(BF16))) * _dot(x, wu_ref[...].astype(BF16))).astype(BF16)
        d = _dot(a, wd_ref[...].astype(BF16))

        @pl.when(j == 0)
        def _():
            o_ref[...] = d

        @pl.when(j > 0)
        def _():
            o_ref[...] += d

    @pl.when(jnp.logical_not(used) & (j == 0))
    def _():
        o_ref[...] = jnp.zeros_like(o_ref)


def _moe(xs, tile_expert, n_used, wg, wu, wd):
    p = xs.shape[0]
    tm = MOE_TILE
    tf = MOE_FF_TILE
    d_ff = wg.shape[2]
    nj = d_ff // tf

    def jj(i, j, nu):
        return jnp.where(i < nu[0], j, nj - 1)

    grid_spec = pltpu.PrefetchScalarGridSpec(
        num_scalar_prefetch=2,
        grid=(p // tm, nj),
        in_specs=[
            pl.BlockSpec((tm, D_MODEL), lambda i, j, te, nu: (i, 0)),
            pl.BlockSpec((None, D_MODEL, tf), lambda i, j, te, nu: (te[i], 0, jj(i, j, nu))),
            pl.BlockSpec((None, D_MODEL, tf), lambda i, j, te, nu: (te[i], 0, jj(i, j, nu))),
            pl.BlockSpec((None, tf, D_MODEL), lambda i, j, te, nu: (te[i], jj(i, j, nu), 0)),
        ],
        out_specs=pl.BlockSpec((tm, D_MODEL), lambda i, j, te, nu: (i, 0)),
        scratch_shapes=[pltpu.VMEM((tm, D_MODEL), BF16)],
    )
    return pl.pallas_call(
        _moe_kernel,
        grid_spec=grid_spec,
        out_shape=jax.ShapeDtypeStruct((p, D_MODEL), F32),
        compiler_params=_params(("arbitrary", "arbitrary")),
        name="moe",
    )(tile_expert, n_used, xs, wg, wu, wd)


def _dispatch(top_i, rank, counts):
    t = top_i.shape[0]
    tm = MOE_TILE
    n_assign = t * TOP_K
    n_tiles = -(-n_assign // tm) + N_EXPERTS
    tiles_e = (counts + tm - 1) // tm
    tile_end = jnp.cumsum(tiles_e)
    tile_start = tile_end - tiles_e
    onehot = (top_i[..., None] == jnp.arange(N_EXPERTS, dtype=jnp.int32)).astype(jnp.int32)
    pos = rank + jnp.sum(onehot * (tile_start * tm), axis=-1)
    n_used = tile_end[-1]
    tile_ids = jnp.arange(n_tiles, dtype=jnp.int32)
    tile_expert = jnp.sum((tile_ids[:, None] >= tile_end[None, :]).astype(jnp.int32), axis=1)
    last_e = jnp.max(jnp.where(tiles_e > 0, jnp.arange(N_EXPERTS, dtype=jnp.int32), 0))
    tile_expert = jnp.where(tile_ids < n_used, tile_expert, last_e).astype(jnp.int32)
    return pos, n_tiles * tm, tile_expert, n_used.reshape(1).astype(jnp.int32)


def _combine_kernel(x_ref, ya_ref, yb_ref, wt_ref, gt_ref, g_ref, y_ref):
    f = wt_ref[:, 0:1] * ya_ref[...] + wt_ref[:, 1:2] * yb_ref[...]
    x = x_ref[...] + gt_ref[...] * f
    y_ref[...] = _rms(x) * g_ref[...]


def _combine(x, yg, row0, wt, mod, tiles_per_mod, norm_g):
    t = x.shape[0]
    tm = TOKEN_TILE
    off = row0 // tm
    row = pl.BlockSpec((tm, D_MODEL), lambda i: (i, 0))
    return pl.pallas_call(
        _combine_kernel,
        grid=(t // tm,),
        in_specs=[row,
                  pl.BlockSpec((tm, D_MODEL), lambda i: (i + off, 0)),
                  pl.BlockSpec((tm, D_MODEL), lambda i: (i + off, 1)),
                  pl.BlockSpec((tm, LANES), lambda i: (i, 0)),
                  _mod_spec(mod, 5, tiles_per_mod), pl.BlockSpec((1, D_MODEL), lambda i: (0, 0))],
        out_specs=row,
        out_shape=jax.ShapeDtypeStruct((t, D_MODEL), F32),
        compiler_params=_params(("arbitrary",)),
        name="combine",
    )(x, yg, yg, wt, mod, norm_g)


def _lane_row(vals):
    return jnp.zeros((1, LANES), F32).at[0, :vals.shape[0]].set(vals.astype(F32))


def _permute_w_in(w):
    c = [CONV_DIM, CONV_DIM + GDN_V_W]
    a0 = c[1]
    q0 = a0 + 2 * GDN_HEADS
    k0 = q0 + SWA_Q_W
    g0 = k0 + 2 * SWA_KV_W
    cols = [w[:, :c[0]], w[:, c[0]:c[1]], w[:, q0:k0], w[:, k0:g0], w[:, g0:],
            w[:, a0:q0], jnp.zeros((D_MODEL, LANES - 2 * GDN_HEADS), w.dtype)]
    return jnp.concatenate(cols, axis=1).astype(BF16)


def kernel(x_prompt, x_sample, c_prompt, c_sample, state_gdn, state_conv, cache_win_k, cache_win_v, w_in, conv_w, a_log, dt_bias, gdn_norm, attn_sinks, w_branch, w_out, w_ada, b_ada, norm_mix, norm_ffn, w_gate_dense, w_up_dense, w_down_dense, w_router, w_gate_moe, w_up_moe, w_down_moe, final_norm):
    bp, lp, _ = x_prompt.shape
    bs, ls, _ = x_sample.shape
    tp, ts = bp * lp, bs * ls
    tm = TOKEN_TILE
    pad_rows = SUBLANES - ls

    m_pad = -(-(bp + bs) // SUBLANES) * SUBLANES
    c_all = jnp.concatenate([c_prompt, c_sample, jnp.zeros((m_pad - bp - bs, D_MODEL), F32)], axis=0)
    ada = _ada(c_all, w_ada, b_ada)

    xp = x_prompt.reshape(tp, D_MODEL)
    xs = x_sample.reshape(ts, D_MODEL)
    outs = {k: [] for k in ("gdn_p", "conv_p", "k_p", "v_p", "gdn_s", "conv_s", "k_s", "v_s")}
    for l in range(DEPTH):
        mod_p = ada[l, :bp].reshape(bp, 1, 6 * D_MODEL)
        mod_s = jnp.repeat(ada[l, bp:bp + bs], ls, axis=0).reshape(ts // tm, tm, 6 * D_MODEL)
        tpm_p = lp // tm
        w_p = _permute_w_in(w_in[l])
        nmix = norm_mix[l].reshape(1, D_MODEL)
        nffn = norm_ffn[l].reshape(1, D_MODEL)
        alog_row, dtb_row = _lane_row(a_log[l]), _lane_row(dt_bias[l])
        sink_row = _lane_row(attn_sinks[l])
        gn_row = gdn_norm[l].reshape(1, GDN_DV)
        wb = w_branch[l].astype(BF16)
        wo = w_out[l].astype(BF16)

        qkv_p, z_p, qb_p, kv_p, gates_p, ab_p = _in_proj(xp, mod_p, tpm_p, nmix, w_p)
        s0_p = jnp.zeros((bp, GDN_HEADS, GDN_DK, GDN_DV), F32)
        o_p, s_p = _gdn(qkv_p, ab_p, s0_p, conv_w[l], alog_row, dtb_row,
                        blk=GDN_CHUNK, lv=GDN_CHUNK, chunks_per_seq=lp // GDN_CHUNK, sb=bp)
        ob_p = _swa(qb_p, kv_p, sink_row, qb=WINDOW, bb=1, blocks_per_seq=lp // WINDOW)
        xp = _mix(o_p, z_p, ob_p, gates_p, xp, mod_p, tpm_p, gn_row, wb, wo)
        outs["gdn_p"].append(s_p)
        outs["conv_p"].append(qkv_p.reshape(bp, lp, CONV_DIM)[:, lp - (CONV_W - 1):])
        kv3 = kv_p.reshape(bp, lp, 2 * SWA_KV_W)[:, lp - WINDOW:]
        outs["k_p"].append(kv3[..., :SWA_KV_W].reshape(bp, WINDOW, SWA_KV_HEADS, SWA_HEAD_DIM))
        outs["v_p"].append(kv3[..., SWA_KV_W:].reshape(bp, WINDOW, SWA_KV_HEADS, SWA_HEAD_DIM))

        qkv_s, z_s, qb_s, kv_s, gates_s, ab_s = _in_proj(xs, mod_s, 1, nmix, w_p)
        qkv_s3 = qkv_s.reshape(bs, ls, CONV_DIM)
        hist = jnp.concatenate([jnp.zeros((bs, SUBLANES - ls - (CONV_W - 1), CONV_DIM), F32),
                                state_conv[l], qkv_s3], axis=1)
        ab_blk = jnp.concatenate([jnp.zeros((bs, pad_rows, LANES), F32), ab_s.reshape(bs, ls, LANES)], axis=1)
        o_s, s_s = _gdn(hist.reshape(bs * SUBLANES, CONV_DIM), ab_blk.reshape(bs * SUBLANES, LANES),
                        state_gdn[l], conv_w[l], alog_row, dtb_row,
                        blk=SUBLANES, lv=ls, chunks_per_seq=1, sb=2)
        o_s = o_s.reshape(bs, SUBLANES, GDN_V_W)[:, pad_rows:].reshape(ts, GDN_V_W)
        pad_after = lambda a: jnp.concatenate(
            [a.reshape(bs, ls, -1), jnp.zeros((bs, pad_rows, a.shape[-1]), F32)], axis=1).reshape(bs * SUBLANES, -1)
        ck = cache_win_k[l].reshape(bs, WINDOW, SWA_KV_W)
        cv = cache_win_v[l].reshape(bs, WINDOW, SWA_KV_W)
        ob_s = _swa(pad_after(qb_s), pad_after(kv_s), sink_row, qb=SUBLANES, bb=8, blocks_per_seq=1,
                    cache_k=ck, cache_v=cv)
        ob_s = ob_s.reshape(bs, SUBLANES, SWA_Q_W)[:, :ls].reshape(ts, SWA_Q_W)
        xs = _mix(o_s, z_s, ob_s, gates_s, xs, mod_s, 1, gn_row, wb, wo)
        outs["gdn_s"].append(s_s)
        outs["conv_s"].append(jnp.concatenate([state_conv[l], qkv_s3], axis=1)[:, -(CONV_W - 1):])
        kv_s3 = kv_s.reshape(bs, ls, 2 * SWA_KV_W)
        knew = jnp.concatenate([ck, kv_s3[..., :SWA_KV_W]], axis=1)[:, -WINDOW:]
        vnew = jnp.concatenate([cv, kv_s3[..., SWA_KV_W:]], axis=1)[:, -WINDOW:]
        outs["k_s"].append(knew.reshape(bs, WINDOW, SWA_KV_HEADS, SWA_HEAD_DIM))
        outs["v_s"].append(vnew.reshape(bs, WINDOW, SWA_KV_HEADS, SWA_HEAD_DIM))

        i = l // 2
        if l % 2 == 0:
            wg, wu, wd = (w_gate_dense[i].astype(BF16), w_up_dense[i].astype(BF16),
                          w_down_dense[i].astype(BF16))
            xp = _ffn(xp, mod_p, tpm_p, nffn, wg, wu, wd)
            xs = _ffn(xs, mod_s.reshape(ts // (2 * tm), 2 * tm, 6 * D_MODEL), 1, nffn, wg, wu, wd)
        else:
            wr_pad = jnp.concatenate([w_router[i], jnp.zeros((D_MODEL, LANES - N_EXPERTS), F32)], axis=1)
            h_p, idx_p, wt_p, rank_p, cnt_p = _route(xp, mod_p, tpm_p, nffn, wr_pad,
                                                     jnp.zeros((SUBLANES, LANES), F32))
            h_s, idx_s, wt_s, rank_s, cnt = _route(xs, mod_s, 1, nffn, wr_pad, cnt_p)
            top_i = jnp.concatenate([idx_p[:, :TOP_K], idx_s[:, :TOP_K]], axis=0)
            rank = jnp.concatenate([rank_p[:, :TOP_K], rank_s[:, :TOP_K]], axis=0)
            pos, p_rows, tile_expert, n_used = _dispatch(top_i, rank, cnt[0, :N_EXPERTS].astype(jnp.int32))
            n_assign = (tp + ts) * TOP_K
            pos_flat = pos.reshape(n_assign)
            xg = jnp.zeros((p_rows, D_MODEL), F32)
            xg = _move_rows(h_p, jnp.arange(tp * TOP_K, dtype=jnp.int32) // TOP_K,
                            pos_flat[:tp * TOP_K], dst_init=xg)
            xg = _move_rows(h_s, jnp.arange(ts * TOP_K, dtype=jnp.int32) // TOP_K,
                            pos_flat[tp * TOP_K:], dst_init=xg)
            y = _moe(xg, tile_expert, n_used, w_gate_moe[i], w_up_moe[i], w_down_moe[i])
            yg = _move_rows(y, pos_flat, jnp.arange(n_assign, dtype=jnp.int32), out_rows=n_assign)
            yg = yg.reshape(tp + ts, TOP_K * D_MODEL)
            fin = final_norm.reshape(1, D_MODEL)
            assert l == DEPTH - 1
            xp = _combine(xp, yg, 0, wt_p, mod_p, tpm_p, fin)
            xs = _combine(xs, yg, tp, wt_s, mod_s, 1, fin)

    st = lambda k: jnp.stack(outs[k])
    return (xp.reshape(bp, lp, D_MODEL), xs.reshape(bs, ls, D_MODEL),
            st("gdn_p"), st("conv_p"), st("k_p"), st("v_p"),
            st("gdn_s"), st("conv_s"), st("k_s"), st("v_s"))
```

```python
import functools

import jax
import jax.numpy as jnp
from jax import lax
from jax.experimental import pallas as pl
from jax.experimental.pallas import tpu as pltpu

F32 = jnp.float32
BF16 = jnp.bfloat16

D_MODEL = 1024
DEPTH = 2
GDN_HEADS = 4
GDN_DK = 128
GDN_DV = 128
CONV_W = 4
SWA_HEADS = 8
SWA_KV_HEADS = 2
SWA_HEAD_DIM = 64
SWA_GROUP = SWA_HEADS // SWA_KV_HEADS
WINDOW = 128
GDN_QK_W = GDN_HEADS * GDN_DK
GDN_V_W = GDN_HEADS * GDN_DV
CONV_DIM = 2 * GDN_QK_W + GDN_V_W
SWA_Q_W = SWA_HEADS * SWA_HEAD_DIM
SWA_KV_W = SWA_KV_HEADS * SWA_HEAD_DIM
N_EXPERTS = 8
TOP_K = 2
EPS = 1e-6

LANES = 128
SUBLANES = 8
VMEM_LIMIT = 56 * 1024 * 1024
NEG_BIG = -1e30

GDN_CHUNK = 64
TOKEN_TILE = 256
MOE_TILE = 1024
MOE_FF_TILE = 512
FFN_FF_TILE = 1408

IN_SPLITS = (CONV_DIM, GDN_V_W, SWA_Q_W, 2 * SWA_KV_W, 2 * D_MODEL, LANES)
IN_PAD_W = sum(IN_SPLITS)


def _silu(x):
    return x * jax.nn.sigmoid(x)


def _softplus(x):
    return jnp.maximum(x, 0.0) + jnp.log1p(jnp.exp(-jnp.abs(x)))


def _dot(a, b):
    return jnp.dot(a, b, preferred_element_type=F32)


def _dot_nt(a, b):
    return lax.dot_general(a, b, (((1,), (1,)), ((), ())), preferred_element_type=F32)


def _dot_tn(a, b):
    return lax.dot_general(a, b, (((0,), (0,)), ((), ())), preferred_element_type=F32)


def _split2(a):
    hi = a.astype(BF16)
    lo = (a - hi.astype(F32)).astype(BF16)
    return hi, lo


def _split3(a):
    hi = a.astype(BF16)
    r = a - hi.astype(F32)
    mid = r.astype(BF16)
    lo = (r - mid.astype(F32)).astype(BF16)
    return hi, mid, lo


def _dot3(a, b):
    ah, al = _split2(a)
    bh, bl = _split2(b)
    return _dot(ah, bh) + (_dot(ah, bl) + _dot(al, bh))


def _dot_exact_lhs(a01, b):
    a = a01.astype(BF16)
    bh, bm, bl = _split3(b)
    return _dot(a, bh) + (_dot(a, bm) + _dot(a, bl))


def _rms(x):
    return x * lax.rsqrt(jnp.mean(x * x, axis=-1, keepdims=True) + EPS)


def _params(sem):
    return pltpu.CompilerParams(dimension_semantics=sem, vmem_limit_bytes=VMEM_LIMIT)


def _ada_kernel(c_ref, w_ref, b_ref, o_ref):
    o_ref[...] = _dot3(_silu(c_ref[...]), w_ref[...]) + b_ref[...]


def _ada(c_all, w_ada, b_ada):
    m = c_all.shape[0]
    tn = 1536
    n = 6 * D_MODEL
    return pl.pallas_call(
        _ada_kernel,
        grid=(DEPTH, n // tn),
        in_specs=[
            pl.BlockSpec((m, D_MODEL), lambda l, j: (0, 0)),
            pl.BlockSpec((None, D_MODEL, tn), lambda l, j: (l, 0, j)),
            pl.BlockSpec((None, 1, tn), lambda l, j: (l, 0, j)),
        ],
        out_specs=pl.BlockSpec((None, m, tn), lambda l, j: (l, 0, j)),
        out_shape=jax.ShapeDtypeStruct((DEPTH, m, n), F32),
        compiler_params=_params(("arbitrary", "arbitrary")),
        name="ada",
    )(c_all, w_ada, b_ada.reshape(DEPTH, 1, n))


def _mod_spec(mod, chunk, tiles_per_mod):
    r = mod.shape[1]
    return pl.BlockSpec((None, r, D_MODEL), lambda i: (i // tiles_per_mod, 0, chunk))


def _in_kernel(x_ref, sh_ref, sc_ref, g_ref, w_ref, *out_refs):
    h = _rms(x_ref[...]) * g_ref[...]
    h = (h * (1.0 + sc_ref[...]) + sh_ref[...]).astype(BF16)
    off = 0
    for ref in out_refs:
        n = ref.shape[-1]
        ref[...] = _dot(h, w_ref[:, off:off + n])
        off += n


def _in_proj(x, mod, tiles_per_mod, norm_g, w_p):
    t = x.shape[0]
    tm = TOKEN_TILE
    return pl.pallas_call(
        _in_kernel,
        grid=(t // tm,),
        in_specs=[
            pl.BlockSpec((tm, D_MODEL), lambda i: (i, 0)),
            _mod_spec(mod, 0, tiles_per_mod),
            _mod_spec(mod, 1, tiles_per_mod),
            pl.BlockSpec((1, D_MODEL), lambda i: (0, 0)),
            pl.BlockSpec((D_MODEL, IN_PAD_W), lambda i: (0, 0)),
        ],
        out_specs=[pl.BlockSpec((tm, n), lambda i: (i, 0)) for n in IN_SPLITS],
        out_shape=[jax.ShapeDtypeStruct((t, n), F32) for n in IN_SPLITS],
        compiler_params=_params(("arbitrary",)),
        name="in_proj",
    )(x, mod, mod, norm_g, w_p)


def _inv_unit_lower(lms, nfac):
    c = lms[0].shape[0]
    r = lax.broadcasted_iota(jnp.int32, (c, c), 0)
    q = lax.broadcasted_iota(jnp.int32, (c, c), 1)
    eye = jnp.where(r == q, 1.0, 0.0)
    ts = [eye - lm for lm in lms]
    ps = list(lms)
    for _ in range(nfac - 1):
        pbs = [p.astype(BF16) for p in ps]
        ps = [_dot(pb, pb) for pb in pbs]
        ts = [t + _dot(t.astype(BF16), p.astype(BF16)) for t, p in zip(ts, ps)]
    splits = [(_split2(t), _split2(lm)) for t, lm in zip(ts, lms)]
    es = [(eye - t) - (_dot(lh, th) + (_dot(lh, tl) + _dot(ll, th)))
          for t, ((th, tl), (lh, ll)) in zip(ts, splits)]
    return [t + _dot(th, e.astype(BF16)) for t, ((th, _), _), e in zip(ts, splits, es)]


GDN_STACK = GDN_HEADS * GDN_CHUNK
GDN_PREP_CHUNKS = 4


def _gdn_prep_kernel(qkv_ref, hist_ref, ab_ref, cw_ref, alog_ref, dtb_ref,
                     u_ref, wq_ref, kdt_ref, qk_ref, egl_ref, xs_ref,
                     *, blk, lv, nfac, chunks_per_seq):
    c = GDN_CHUNK
    hc = GDN_STACK
    cb = GDN_PREP_CHUNKS
    hist = SUBLANES
    sh = blk.bit_length() - 1
    first = ((pl.program_id(0) * cb) % chunks_per_seq) == 0

    @pl.when(first)
    def _():
        xs_ref[0:hist, :] = jnp.zeros((hist, CONV_DIM), F32)

    @pl.when(jnp.logical_not(first))
    def _():
        xs_ref[0:hist, :] = hist_ref[...]

    xs_ref[hist:hist + cb * c, :] = qkv_ref[...]

    ri = lax.broadcasted_iota(jnp.int32, (hc, hc), 0)
    cj = lax.broadcasted_iota(jnp.int32, (hc, hc), 1)
    same = (ri >> sh) == (cj >> sh)
    incl = same & (ri >= cj)
    strict = same & (ri > cj)
    rt = lax.broadcasted_iota(jnp.int32, (c, c), 0)
    ct = lax.broadcasted_iota(jnp.int32, (c, c), 1)
    same_t = (rt >> sh) == (ct >> sh)
    cum_t = jnp.where(same_t & (rt >= ct), 1.0, 0.0)
    tot_t = jnp.where(same_t, 1.0, 0.0)
    if lv < blk:
        rvalid = (lax.broadcasted_iota(jnp.int32, (c, 1), 0) & (blk - 1)) >= (blk - lv)

    def stack_cols(a, lane0):
        return jnp.concatenate([a[:, lane0 + h:lane0 + h + 1] for h in range(GDN_HEADS)], axis=0)

    lms, rhs, qds, gtots = [], [], [], []
    for ci in range(cb):
        base = hist + ci * c

        def conv(c0, base=base):
            y = xs_ref[base - 3:base - 3 + c, c0:c0 + LANES] * cw_ref[0:1, c0:c0 + LANES]
            for i in range(1, CONV_W):
                y = y + xs_ref[base - 3 + i:base - 3 + i + c, c0:c0 + LANES] * cw_ref[i:i + 1, c0:c0 + LANES]
            y = _silu(y)
            return jnp.where(rvalid, y, 0.0) if lv < blk else y

        ab = ab_ref[ci * c:(ci + 1) * c, :]
        g_all = -jnp.exp(alog_ref[...]) * _softplus(ab + dtb_ref[...])
        beta_all = jax.nn.sigmoid(ab)
        if lv < blk:
            g_all = jnp.where(rvalid, g_all, 0.0)
            beta_all = jnp.where(rvalid, beta_all, 0.0)
        gcum = _dot_exact_lhs(cum_t, g_all)
        gtot = _dot_exact_lhs(tot_t, g_all)
        gc = stack_cols(gcum, 0)
        gl = stack_cols(gtot, 0)
        beta = stack_cols(beta_all, GDN_HEADS)
        gcb = jnp.broadcast_to(gc, (hc, LANES))
        gr = jnp.concatenate([gcb[0:LANES].T[0:1], gcb[LANES:].T[0:1]], axis=1)

        q = jnp.concatenate([conv(h * GDN_DK) for h in range(GDN_HEADS)], axis=0)
        k = jnp.concatenate([conv(GDN_QK_W + h * GDN_DK) for h in range(GDN_HEADS)], axis=0)
        v = jnp.concatenate([conv(2 * GDN_QK_W + h * GDN_DV) for h in range(GDN_HEADS)], axis=0)
        q = q * lax.rsqrt(jnp.sum(q * q, axis=-1, keepdims=True) + EPS) * (GDN_DK ** -0.5)
        k = k * lax.rsqrt(jnp.sum(k * k, axis=-1, keepdims=True) + EPS)

        decay = jnp.where(incl, jnp.exp(jnp.minimum(gc - gr, 0.0)), 0.0)
        kb = k * beta
        k16 = k.astype(BF16)
        lms.append(jnp.where(strict, _dot_nt(kb.astype(BF16), k16) * decay, 0.0))
        eg = jnp.exp(gc)
        rhs.append(_split2(jnp.concatenate([v * beta, kb * eg], axis=1)))
        qds.append(q * eg)
        gtots.append(gtot)
        kd = k * jnp.exp(gl - gc)
        qk_ref[ci] = (_dot_nt(q.astype(BF16), k16) * decay).astype(BF16)
        kdt_ref[ci] = kd.T.astype(BF16)

    tinvs = _inv_unit_lower(lms, nfac)
    for ci in range(cb):
        t16 = tinvs[ci].astype(BF16)
        rh, rl = rhs[ci]
        x = _dot(t16, rh) + _dot(t16, rl)
        w = x[:, GDN_DV:]
        for h in range(GDN_HEADS):
            sl = slice(h * c, (h + 1) * c)
            u_ref[ci * c:(ci + 1) * c, h * GDN_DV:(h + 1) * GDN_DV] = x[sl, :GDN_DV]
            wq_ref[ci, h] = jnp.concatenate([w[sl], qds[ci][sl]], axis=0).astype(BF16)
            egl = jnp.exp(jnp.broadcast_to(gtots[ci][:, h:h + 1], (c, LANES)))
            egl_ref[ci, h] = egl.reshape(c // SUBLANES, SUBLANES, LANES)[:, 0, :]


def _gdn_scan_kernel(u_ref, wq_ref, kdt_ref, qk_ref, egl_ref, s0_ref, o_ref, s_ref, *, blk, sb):
    c = GDN_CHUNK
    nb = c // blk
    sh = blk.bit_length() - 1
    n = pl.program_id(1)

    @pl.when(n == 0)
    def _():
        s_ref[...] = s0_ref[...]

    rblk2 = (lax.broadcasted_iota(jnp.int32, (2 * c, 1), 0) & (c - 1)) >> sh
    rblk_s = (lax.broadcasted_iota(jnp.int32, (GDN_STACK, 1), 0) & (c - 1)) >> sh
    rrs = []
    for b in range(sb):
        for h in range(GDN_HEADS):
            wq = wq_ref[b, h]
            rr = _dot(wq, s_ref[b, 0, h].astype(BF16))
            for j in range(1, nb):
                rr = jnp.where(rblk2 == j, _dot(wq, s_ref[b, j, h].astype(BF16)), rr)
            rrs.append(rr)
    vnews = []
    for b in range(sb):
        rb = rrs[b * GDN_HEADS:(b + 1) * GDN_HEADS]
        vnew = jnp.concatenate([u_ref[b, :, h * GDN_DV:(h + 1) * GDN_DV] - rb[h][:c]
                                for h in range(GDN_HEADS)], axis=0)
        vnews.append(vnew)
        o = jnp.concatenate([r[c:] for r in rb], axis=0) + _dot(qk_ref[b], vnew.astype(BF16))
        for h in range(GDN_HEADS):
            o_ref[b, :, h * GDN_DV:(h + 1) * GDN_DV] = o[h * c:(h + 1) * c]
    for b in range(sb):
        vnew = vnews[b]
        for j in range(nb):
            vj = (vnew if nb == 1 else jnp.where(rblk_s == j, vnew, 0.0)).astype(BF16)
            zero = jnp.zeros((c, GDN_DV), BF16)
            vbd = jnp.concatenate(
                [jnp.concatenate([vj[h * c:(h + 1) * c] if g == h else zero for g in range(GDN_HEADS)], axis=1)
                 for h in range(GDN_HEADS)], axis=0)
            upd = _dot(kdt_ref[b], vbd)
            for h in range(GDN_HEADS):
                s_ref[b, j, h] = (egl_ref[b, h, j:j + 1, :] * s_ref[b, j, h]
                                  + upd[:, h * GDN_DV:(h + 1) * GDN_DV])


def _gdn(qkv, ab, s0, conv_w, alog_row, dtb_row, *, blk, lv, chunks_per_seq, sb):
    rows = qkv.shape[0]
    c = GDN_CHUNK
    hc = GDN_STACK
    cb = GDN_PREP_CHUNKS
    nb = c // blk
    assert blk in (SUBLANES, c) and lv <= blk
    nfac = max(1, (lv - 1).bit_length())
    nch = rows // c
    nseq = nch // chunks_per_seq
    assert nseq * nb == s0.shape[0] and nch % cb == 0
    assert chunks_per_seq == 1 or chunks_per_seq % cb == 0
    hist_blocks = cb * c // SUBLANES
    prep = functools.partial(_gdn_prep_kernel, blk=blk, lv=lv, nfac=nfac, chunks_per_seq=chunks_per_seq)
    u, wq, kdt, qk, egl = pl.pallas_call(
        prep,
        grid=(nch // cb,),
        in_specs=[
            pl.BlockSpec((cb * c, CONV_DIM), lambda i: (i, 0)),
            pl.BlockSpec((SUBLANES, CONV_DIM), lambda i: (jnp.maximum(i * hist_blocks - 1, 0), 0)),
            pl.BlockSpec((cb * c, LANES), lambda i: (i, 0)),
            pl.BlockSpec((CONV_W, CONV_DIM), lambda i: (0, 0)),
            pl.BlockSpec((1, LANES), lambda i: (0, 0)),
            pl.BlockSpec((1, LANES), lambda i: (0, 0)),
        ],
        out_specs=[
            pl.BlockSpec((cb * c, GDN_V_W), lambda i: (i, 0)),
            pl.BlockSpec((cb, GDN_HEADS, 2 * c, GDN_DK), lambda i: (i, 0, 0, 0)),
            pl.BlockSpec((cb, GDN_DK, hc), lambda i: (i, 0, 0)),
            pl.BlockSpec((cb, hc, hc), lambda i: (i, 0, 0)),
            pl.BlockSpec((cb, GDN_HEADS, SUBLANES, LANES), lambda i: (i, 0, 0, 0)),
        ],
        out_shape=[
            jax.ShapeDtypeStruct((rows, GDN_V_W), F32),
            jax.ShapeDtypeStruct((nch, GDN_HEADS, 2 * c, GDN_DK), BF16),
            jax.ShapeDtypeStruct((nch, GDN_DK, hc), BF16),
            jax.ShapeDtypeStruct((nch, hc, hc), BF16),
            jax.ShapeDtypeStruct((nch, GDN_HEADS, SUBLANES, LANES), F32),
        ],
        scratch_shapes=[pltpu.VMEM((SUBLANES + cb * c, CONV_DIM), F32)],
        compiler_params=_params(("arbitrary",)),
        name="gdn_prep",
    )(qkv, qkv, ab, conv_w, alog_row, dtb_row)

    n = chunks_per_seq
    scan = functools.partial(_gdn_scan_kernel, blk=blk, sb=sb)
    seq5 = lambda a: a.reshape((nseq, n) + a.shape[1:])
    s0g = s0.reshape(nseq, nb, GDN_HEADS, GDN_DK, GDN_DV)
    o, s_new = pl.pallas_call(
        scan,
        grid=(nseq // sb, n),
        in_specs=[
            pl.BlockSpec((sb, c, GDN_V_W), lambda i, t: (i, t, 0)),
            pl.BlockSpec((sb, None, GDN_HEADS, 2 * c, GDN_DK), lambda i, t: (i, t, 0, 0, 0)),
            pl.BlockSpec((sb, None, GDN_DK, hc), lambda i, t: (i, t, 0, 0)),
            pl.BlockSpec((sb, None, hc, hc), lambda i, t: (i, t, 0, 0)),
            pl.BlockSpec((sb, None, GDN_HEADS, SUBLANES, LANES), lambda i, t: (i, t, 0, 0, 0)),
            pl.BlockSpec((sb, nb, GDN_HEADS, GDN_DK, GDN_DV), lambda i, t: (i, 0, 0, 0, 0)),
        ],
        out_specs=[
            pl.BlockSpec((sb, c, GDN_V_W), lambda i, t: (i, t, 0)),
            pl.BlockSpec((sb, nb, GDN_HEADS, GDN_DK, GDN_DV), lambda i, t: (i, 0, 0, 0, 0)),
        ],
        out_shape=[
            jax.ShapeDtypeStruct((nseq, n * c, GDN_V_W), F32),
            jax.ShapeDtypeStruct(s0g.shape, F32),
        ],
        compiler_params=_params(("arbitrary", "arbitrary")),
        name="gdn_scan",
    )(u.reshape(nseq, n * c, GDN_V_W), seq5(wq), seq5(kdt), seq5(qk), seq5(egl), s0g)
    return o.reshape(rows, GDN_V_W), s_new.reshape(s0.shape)


def _swa_kernel(q_ref, kc_ref, vc_ref, kp_ref, vp_ref, sink_ref, o_ref, *, qb, bb, prev_is_cache):
    w = WINDOW
    n = pl.program_id(1)
    rows = SWA_GROUP * qb
    qi = lax.broadcasted_iota(jnp.int32, (rows, 2 * w), 0) & (qb - 1)
    kj = lax.broadcasted_iota(jnp.int32, (rows, 2 * w), 1)
    ok = ((kj < w) & (kj >= qi)) | ((kj >= w) & (kj - w <= qi))
    if not prev_is_cache:
        ok = ok & ((kj >= w) | (n > 0))
    scale = SWA_HEAD_DIM ** -0.5

    def pad_rows(a):
        return a if qb == w else jnp.concatenate([a, jnp.zeros((w - qb, a.shape[1]), a.dtype)], axis=0)

    for b in range(bb):
        r0 = b * qb
        kc_all = pad_rows(kc_ref[r0:r0 + qb, :])
        vc_all = pad_rows(vc_ref[r0:r0 + qb, :])
        kp_all = kp_ref[b] if prev_is_cache else kp_ref[...]
        vp_all = vp_ref[b] if prev_is_cache else vp_ref[...]
        ss, sinks, vs = [], [], []
        ones = jnp.ones((2 * w, SWA_HEAD_DIM), BF16)
        for hk in range(SWA_KV_HEADS):
            c0 = hk * SWA_HEAD_DIM
            sl = slice(c0, c0 + SWA_HEAD_DIM)
            k_all = jnp.concatenate([kp_all[:, sl], kc_all[:, sl]], axis=0).astype(BF16)
            v_all = jnp.concatenate([vp_all[:, sl], vc_all[:, sl]], axis=0).astype(BF16)
            vs.append(jnp.concatenate([v_all, ones], axis=1))
            heads = [hk * SWA_GROUP + g for g in range(SWA_GROUP)]
            q = jnp.concatenate(
                [q_ref[r0:r0 + qb, hd * SWA_HEAD_DIM:(hd + 1) * SWA_HEAD_DIM] for hd in heads],
                axis=0).astype(BF16)
            sinks.append(jnp.concatenate(
                [jnp.broadcast_to(sink_ref[0:1, hd:hd + 1], (qb, 1)) for hd in heads], axis=0))
            ss.append(jnp.where(ok, _dot_nt(q, k_all) * scale, NEG_BIG))
        ms = [jnp.maximum(jnp.max(s, axis=-1, keepdims=True), sink) for s, sink in zip(ss, sinks)]
        accs = [_dot(jnp.exp(s - m).astype(BF16), v) for s, m, v in zip(ss, ms, vs)]
        for hk in range(SWA_KV_HEADS):
            acc = accs[hk]
            den = acc[:, SWA_HEAD_DIM:SWA_HEAD_DIM + 1] + jnp.exp(sinks[hk] - ms[hk])
            res = acc[:, :SWA_HEAD_DIM] / den
            for g in range(SWA_GROUP):
                hd = hk * SWA_GROUP + g
                o_ref[r0:r0 + qb, hd * SWA_HEAD_DIM:(hd + 1) * SWA_HEAD_DIM] = res[g * qb:(g + 1) * qb]


def _swa(q, kv, sink_row, *, qb, bb, blocks_per_seq, cache_k=None, cache_v=None):
    rows = q.shape[0]
    w = WINDOW
    prev_is_cache = cache_k is not None
    steps = rows // (qb * bb)
    nseq = steps // blocks_per_seq
    kern = functools.partial(_swa_kernel, qb=qb, bb=bb, prev_is_cache=prev_is_cache)
    cur = lambda i, n: (i * blocks_per_seq + n, 0)
    if prev_is_cache:
        kp_spec = pl.BlockSpec((bb, w, SWA_KV_W), lambda i, n: (i, 0, 0))
        vp_spec = kp_spec
        kp_arr, vp_arr = cache_k, cache_v
    else:
        assert qb == w and bb == 1
        kp_spec = pl.BlockSpec((w, SWA_KV_W), lambda i, n: (i * blocks_per_seq + jnp.maximum(n - 1, 0), 0))
        vp_spec = pl.BlockSpec((w, SWA_KV_W), lambda i, n: (i * blocks_per_seq + jnp.maximum(n - 1, 0), 1))
        kp_arr, vp_arr = kv, kv
    return pl.pallas_call(
        kern,
        grid=(nseq, blocks_per_seq),
        in_specs=[
            pl.BlockSpec((qb * bb, SWA_Q_W), cur),
            pl.BlockSpec((qb * bb, SWA_KV_W), cur),
            pl.BlockSpec((qb * bb, SWA_KV_W), lambda i, n: (i * blocks_per_seq + n, 1)),
            kp_spec,
            vp_spec,
            pl.BlockSpec((1, LANES), lambda i, n: (0, 0)),
        ],
        out_specs=pl.BlockSpec((qb * bb, SWA_Q_W), cur),
        out_shape=jax.ShapeDtypeStruct((rows, SWA_Q_W), F32),
        compiler_params=_params(("arbitrary", "arbitrary")),
        name="swa",
    )(q, kv, kv, kp_arr, vp_arr, sink_row)


def _mix_kernel(o_ref, z_ref, ob_ref, gates_ref, x_ref, gt_ref, gn_ref, wb_ref, wo_ref, y_ref):
    parts = []
    for h in range(GDN_HEADS):
        sl = slice(h * GDN_DV, (h + 1) * GDN_DV)
        parts.append(_rms(o_ref[:, sl]) * gn_ref[...] * _silu(z_ref[:, sl]))
    oa = jnp.concatenate(parts, axis=1).astype(BF16)
    pa = _dot(oa, wb_ref[0:GDN_V_W, :])
    pb = _dot(ob_ref[...].astype(BF16), wb_ref[GDN_V_W:, :])
    mixed = (jax.nn.sigmoid(gates_ref[:, 0:D_MODEL]) * pa
             + jax.nn.sigmoid(gates_ref[:, D_MODEL:]) * pb)
    y_ref[...] = x_ref[...] + gt_ref[...] * _dot(mixed.astype(BF16), wo_ref[...])


def _mix(o_raw, z, ob, gates, x, mod, tiles_per_mod, gn_row, wb, wo):
    t = x.shape[0]
    tm = TOKEN_TILE
    row = lambda n: pl.BlockSpec((tm, n), lambda i: (i, 0))
    full = lambda a: pl.BlockSpec(a.shape, lambda i: (0, 0))
    return pl.pallas_call(
        _mix_kernel,
        grid=(t // tm,),
        in_specs=[row(GDN_V_W), row(GDN_V_W), row(SWA_Q_W), row(2 * D_MODEL), row(D_MODEL),
                  _mod_spec(mod, 2, tiles_per_mod), full(gn_row), full(wb), full(wo)],
        out_specs=row(D_MODEL),
        out_shape=jax.ShapeDtypeStruct((t, D_MODEL), F32),
        compiler_params=_params(("arbitrary",)),
        name="mix",
    )(o_raw, z, ob, gates, x, mod, gn_row, wb, wo)


def _ffn_kernel(x_ref, sh_ref, sc_ref, gt_ref, g_ref, wg_ref, wu_ref, wd_ref, y_ref, h_ref, acc_ref):
    j = pl.program_id(1)

    @pl.when(j == 0)
    def _():
        h = _rms(x_ref[...]) * g_ref[...]
        h_ref[...] = (h * (1.0 + sc_ref[...]) + sh_ref[...]).astype(BF16)

    h = h_ref[...]
    a = (_silu(_dot(h, wg_ref[...])) * _dot(h, wu_ref[...])).astype(BF16)
    d = _dot(a, wd_ref[...])

    @pl.when(j == 0)
    def _():
        acc_ref[...] = d

    @pl.when(j > 0)
    def _():
        acc_ref[...] += d

    @pl.when(j == pl.num_programs(1) - 1)
    def _():
        y_ref[...] = x_ref[...] + gt_ref[...] * acc_ref[...]


def _ffn(x, mod, tiles_per_mod, norm_g, wg, wu, wd):
    t = x.shape[0]
    tm = 2 * TOKEN_TILE
    tpm = max(1, tiles_per_mod // 2)
    d_ff = wg.shape[1]
    tf = FFN_FF_TILE
    r = mod.shape[1]
    mspec = lambda chunk: pl.BlockSpec((None, r, D_MODEL), lambda i, j: (i // tpm, 0, chunk))
    return pl.pallas_call(
        _ffn_kernel,
        grid=(t // tm, d_ff // tf),
        in_specs=[
            pl.BlockSpec((tm, D_MODEL), lambda i, j: (i, 0)),
            mspec(3), mspec(4), mspec(5),
            pl.BlockSpec((1, D_MODEL), lambda i, j: (0, 0)),
            pl.BlockSpec((D_MODEL, tf), lambda i, j: (0, j)),
            pl.BlockSpec((D_MODEL, tf), lambda i, j: (0, j)),
            pl.BlockSpec((tf, D_MODEL), lambda i, j: (j, 0)),
        ],
        out_specs=pl.BlockSpec((tm, D_MODEL), lambda i, j: (i, 0)),
        out_shape=jax.ShapeDtypeStruct((t, D_MODEL), F32),
        scratch_shapes=[pltpu.VMEM((tm, D_MODEL), BF16), pltpu.VMEM((tm, D_MODEL), F32)],
        compiler_params=_params(("arbitrary", "arbitrary")),
        name="ffn",
    )(x, mod, mod, mod, norm_g, wg, wu, wd)


def _route_kernel(x_ref, sh_ref, sc_ref, g_ref, wr_ref, base_ref,
                  h_ref, idx_ref, wt_ref, rank_ref, cnt_ref, run_ref):
    @pl.when(pl.program_id(0) == 0)
    def _():
        run_ref[...] = base_ref[...]

    h = _rms(x_ref[...]) * g_ref[...]
    h = h * (1.0 + sc_ref[...]) + sh_ref[...]
    h_ref[...] = h
    logits = _dot3(h, wr_ref[...])
    lane_i = lax.broadcasted_iota(jnp.int32, logits.shape, 1)
    lane = lane_i.astype(F32)
    logits = jnp.where(lane_i < N_EXPERTS, logits, NEG_BIG)
    m1 = jnp.max(logits, axis=-1, keepdims=True)
    i1 = jnp.min(jnp.where(logits == m1, lane, float(LANES)), axis=-1, keepdims=True)
    rest = jnp.where(lane == i1, NEG_BIG, logits)
    m2 = jnp.max(rest, axis=-1, keepdims=True)
    i2 = jnp.min(jnp.where(rest == m2, lane, float(LANES)), axis=-1, keepdims=True)
    e = jnp.exp(m2 - m1)
    den = 1.0 + e
    idx_ref[...] = jnp.where(lane_i == 0, i1, jnp.where(lane_i == 1, i2, 0.0)).astype(jnp.int32)
    wt_ref[...] = jnp.where(lane_i == 0, 1.0 / den, jnp.where(lane_i == 1, e / den, 0.0))
    tm = logits.shape[0]
    oh1 = lane == i1
    oh2 = lane == i2
    cmat = jnp.where(oh1 | oh2, 1.0, 0.0)
    rt = lax.broadcasted_iota(jnp.int32, (tm, tm), 0)
    ct = lax.broadcasted_iota(jnp.int32, (tm, tm), 1)
    before = _dot(jnp.where(rt > ct, 1.0, 0.0).astype(BF16), cmat.astype(BF16)) + run_ref[0:1, :]
    r1 = jnp.sum(jnp.where(oh1, before, 0.0), axis=-1, keepdims=True)
    r2 = jnp.sum(jnp.where(oh2, before, 0.0), axis=-1, keepdims=True)
    rank_ref[...] = jnp.where(lane_i == 0, r1, jnp.where(lane_i == 1, r2, 0.0)).astype(jnp.int32)
    run_ref[0:1, :] = run_ref[0:1, :] + jnp.sum(cmat, axis=0, keepdims=True)
    cnt_ref[...] = run_ref[...]


def _route(x, mod, tiles_per_mod, norm_g, wr_pad, base):
    t = x.shape[0]
    tm = TOKEN_TILE
    assert tm <= 256
    row = lambda n: pl.BlockSpec((tm, n), lambda i: (i, 0))
    cst = lambda a, b: pl.BlockSpec((a, b), lambda i: (0, 0))
    return pl.pallas_call(
        _route_kernel,
        grid=(t // tm,),
        in_specs=[row(D_MODEL), _mod_spec(mod, 3, tiles_per_mod), _mod_spec(mod, 4, tiles_per_mod),
                  cst(1, D_MODEL), cst(D_MODEL, LANES), cst(SUBLANES, LANES)],
        out_specs=[row(D_MODEL), row(LANES), row(LANES), row(LANES), cst(SUBLANES, LANES)],
        out_shape=[jax.ShapeDtypeStruct((t, D_MODEL), F32),
                   jax.ShapeDtypeStruct((t, LANES), jnp.int32),
                   jax.ShapeDtypeStruct((t, LANES), F32),
                   jax.ShapeDtypeStruct((t, LANES), jnp.int32),
                   jax.ShapeDtypeStruct((SUBLANES, LANES), F32)],
        scratch_shapes=[pltpu.VMEM((SUBLANES, LANES), F32)],
        compiler_params=_params(("arbitrary",)),
        name="route",
    )(x, mod, mod, norm_g, wr_pad, base)


def _moe_kernel(te_ref, nu_ref, x_ref, wg_ref, wu_ref, wd_ref, o_ref, xb_ref):
    i = pl.program_id(0)
    j = pl.program_id(1)
    used = i < nu_ref[0]

    @pl.when(used & (j == 0))
    def _():
        xb_ref[...] = x_ref[...].astype(BF16)

    @pl.when(used)
    def _():
        x = xb_ref[...]
        a = (_silu(_dot(x, wg_ref[...].astype(BF16))) * _dot(x, wu_ref[...].astype(BF16))).astype(BF16)
        d = _dot(a, wd_ref[...].astype(BF16))

        @pl.when(j == 0)
        def _():
            o_ref[...] = d

        @pl.when(j > 0)
        def _():
            o_ref[...] += d

    @pl.when(jnp.logical_not(used) & (j == 0))
    def _():
        o_ref[...] = jnp.zeros_like(o_ref)


def _moe(xs, tile_expert, n_used, wg, wu, wd):
    p = xs.shape[0]
    tm = MOE_TILE
    tf = MOE_FF_TILE
    d_ff = wg.shape[2]
    nj = d_ff // tf

    def jj(i, j, nu):
        return jnp.where(i < nu[0], j, nj - 1)

    grid_spec = pltpu.PrefetchScalarGridSpec(
        num_scalar_prefetch=2,
        grid=(p // tm, nj),
        in_specs=[
            pl.BlockSpec((tm, D_MODEL), lambda i, j, te, nu: (i, 0)),
            pl.BlockSpec((None, D_MODEL, tf), lambda i, j, te, nu: (te[i], 0, jj(i, j, nu))),
            pl.BlockSpec((None, D_MODEL, tf), lambda i, j, te, nu: (te[i], 0, jj(i, j, nu))),
            pl.BlockSpec((None, tf, D_MODEL), lambda i, j, te, nu: (te[i], jj(i, j, nu), 0)),
        ],
        out_specs=pl.BlockSpec((tm, D_MODEL), lambda i, j, te, nu: (i, 0)),
        scratch_shapes=[pltpu.VMEM((tm, D_MODEL), BF16)],
    )
    return pl.pallas_call(
        _moe_kernel,
        grid_spec=grid_spec,
        out_shape=jax.ShapeDtypeStruct((p, D_MODEL), F32),
        compiler_params=_params(("arbitrary", "arbitrary")),
        name="moe",
    )(tile_expert, n_used, xs, wg, wu, wd)


def _dispatch(top_i, rank, counts):
    t = top_i.shape[0]
    tm = MOE_TILE
    n_assign = t * TOP_K
    n_tiles = -(-n_assign // tm) + N_EXPERTS
    tiles_e = (counts + tm - 1) // tm
    tile_end = jnp.cumsum(tiles_e)
    tile_start = tile_end - tiles_e
    onehot = (top_i[..., None] == jnp.arange(N_EXPERTS, dtype=jnp.int32)).astype(jnp.int32)
    pos = rank + jnp.sum(onehot * (tile_start * tm), axis=-1)
    n_used = tile_end[-1]
    tile_ids = jnp.arange(n_tiles, dtype=jnp.int32)
    tile_expert = jnp.sum((tile_ids[:, None] >= tile_end[None, :]).astype(jnp.int32), axis=1)
    last_e = jnp.max(jnp.where(tiles_e > 0, jnp.arange(N_EXPERTS, dtype=jnp.int32), 0))
    tile_expert = jnp.where(tile_ids < n_used, tile_expert, last_e).astype(jnp.int32)
    return pos, n_tiles * tm, tile_expert, n_used.reshape(1).astype(jnp.int32)


def _combine_kernel(x_ref, ya_ref, yb_ref, wt_ref, gt_ref, g_ref, y_ref):
    f = wt_ref[:, 0:1] * ya_ref[...] + wt_ref[:, 1:2] * yb_ref[...]
    x = x_ref[...] + gt_ref[...] * f
    y_ref[...] = _rms(x) * g_ref[...]


def _combine(x, yg, row0, wt, mod, tiles_per_mod, norm_g):
    t = x.shape[0]
    tm = TOKEN_TILE
    off = row0 // tm
    row = pl.BlockSpec((tm, D_MODEL), lambda i: (i, 0))
    return pl.pallas_call(
        _combine_kernel,
        grid=(t // tm,),
        in_specs=[row,
                  pl.BlockSpec((tm, D_MODEL), lambda i: (i + off, 0)),
                  pl.BlockSpec((tm, D_MODEL), lambda i: (i + off, 1)),
                  pl.BlockSpec((tm, LANES), lambda i: (i, 0)),
                  _mod_spec(mod, 5, tiles_per_mod), pl.BlockSpec((1, D_MODEL), lambda i: (0, 0))],
        out_specs=row,
        out_shape=jax.ShapeDtypeStruct((t, D_MODEL), F32),
        compiler_params=_params(("arbitrary",)),
        name="combine",
    )(x, yg, yg, wt, mod, norm_g)


def _lane_row(vals):
    return jnp.zeros((1, LANES), F32).at[0, :vals.shape[0]].set(vals.astype(F32))


def _permute_w_in(w):
    c = [CONV_DIM, CONV_DIM + GDN_V_W]
    a0 = c[1]
    q0 = a0 + 2 * GDN_HEADS
    k0 = q0 + SWA_Q_W
    g0 = k0 + 2 * SWA_KV_W
    cols = [w[:, :c[0]], w[:, c[0]:c[1]], w[:, q0:k0], w[:, k0:g0], w[:, g0:],
            w[:, a0:q0], jnp.zeros((D_MODEL, LANES - 2 * GDN_HEADS), w.dtype)]
    return jnp.concatenate(cols, axis=1).astype(BF16)


def kernel(x_prompt, x_sample, c_prompt, c_sample, state_gdn, state_conv, cache_win_k, cache_win_v, w_in, conv_w, a_log, dt_bias, gdn_norm, attn_sinks, w_branch, w_out, w_ada, b_ada, norm_mix, norm_ffn, w_gate_dense, w_up_dense, w_down_dense, w_router, w_gate_moe, w_up_moe, w_down_moe, final_norm):
    bp, lp, _ = x_prompt.shape
    bs, ls, _ = x_sample.shape
    tp, ts = bp * lp, bs * ls
    tm = TOKEN_TILE
    pad_rows = SUBLANES - ls

    m_pad = -(-(bp + bs) // SUBLANES) * SUBLANES
    c_all = jnp.concatenate([c_prompt, c_sample, jnp.zeros((m_pad - bp - bs, D_MODEL), F32)], axis=0)
    ada = _ada(c_all, w_ada, b_ada)

    xp = x_prompt.reshape(tp, D_MODEL)
    xs = x_sample.reshape(ts, D_MODEL)
    outs = {k: [] for k in ("gdn_p", "conv_p", "k_p", "v_p", "gdn_s", "conv_s", "k_s", "v_s")}
    for l in range(DEPTH):
        mod_p = ada[l, :bp].reshape(bp, 1, 6 * D_MODEL)
        mod_s = jnp.repeat(ada[l, bp:bp + bs], ls, axis=0).reshape(ts // tm, tm, 6 * D_MODEL)
        tpm_p = lp // tm
        w_p = _permute_w_in(w_in[l])
        nmix = norm_mix[l].reshape(1, D_MODEL)
        nffn = norm_ffn[l].reshape(1, D_MODEL)
        alog_row, dtb_row = _lane_row(a_log[l]), _lane_row(dt_bias[l])
        sink_row = _lane_row(attn_sinks[l])
        gn_row = gdn_norm[l].reshape(1, GDN_DV)
        wb = w_branch[l].astype(BF16)
        wo = w_out[l].astype(BF16)

        qkv_p, z_p, qb_p, kv_p, gates_p, ab_p = _in_proj(xp, mod_p, tpm_p, nmix, w_p)
        s0_p = jnp.zeros((bp, GDN_HEADS, GDN_DK, GDN_DV), F32)
        o_p, s_p = _gdn(qkv_p, ab_p, s0_p, conv_w[l], alog_row, dtb_row,
                        blk=GDN_CHUNK, lv=GDN_CHUNK, chunks_per_seq=lp // GDN_CHUNK, sb=bp)
        ob_p = _swa(qb_p, kv_p, sink_row, qb=WINDOW, bb=1, blocks_per_seq=lp // WINDOW)
        xp = _mix(o_p, z_p, ob_p, gates_p, xp, mod_p, tpm_p, gn_row, wb, wo)
        outs["gdn_p"].append(s_p)
        outs["conv_p"].append(qkv_p.reshape(bp, lp, CONV_DIM)[:, lp - (CONV_W - 1):])
        kv3 = kv_p.reshape(bp, lp, 2 * SWA_KV_W)[:, lp - WINDOW:]
        outs["k_p"].append(kv3[..., :SWA_KV_W].reshape(bp, WINDOW, SWA_KV_HEADS, SWA_HEAD_DIM))
        outs["v_p"].append(kv3[..., SWA_KV_W:].reshape(bp, WINDOW, SWA_KV_HEADS, SWA_HEAD_DIM))

        qkv_s, z_s, qb_s, kv_s, gates_s, ab_s = _in_proj(xs, mod_s, 1, nmix, w_p)
        qkv_s3 = qkv_s.reshape(bs, ls, CONV_DIM)
        hist = jnp.concatenate([jnp.zeros((bs, SUBLANES - ls - (CONV_W - 1), CONV_DIM), F32),
                                state_conv[l], qkv_s3], axis=1)
        ab_blk = jnp.concatenate([jnp.zeros((bs, pad_rows, LANES), F32), ab_s.reshape(bs, ls, LANES)], axis=1)
        o_s, s_s = _gdn(hist.reshape(bs * SUBLANES, CONV_DIM), ab_blk.reshape(bs * SUBLANES, LANES),
                        state_gdn[l], conv_w[l], alog_row, dtb_row,
                        blk=SUBLANES, lv=ls, chunks_per_seq=1, sb=2)
        o_s = o_s.reshape(bs, SUBLANES, GDN_V_W)[:, pad_rows:].reshape(ts, GDN_V_W)
        pad_after = lambda a: jnp.concatenate(
            [a.reshape(bs, ls, -1), jnp.zeros((bs, pad_rows, a.shape[-1]), F32)], axis=1).reshape(bs * SUBLANES, -1)
        ck = cache_win_k[l].reshape(bs, WINDOW, SWA_KV_W)
        cv = cache_win_v[l].reshape(bs, WINDOW, SWA_KV_W)
        ob_s = _swa(pad_after(qb_s), pad_after(kv_s), sink_row, qb=SUBLANES, bb=8, blocks_per_seq=1,
                    cache_k=ck, cache_v=cv)
        ob_s = ob_s.reshape(bs, SUBLANES, SWA_Q_W)[:, :ls].reshape(ts, SWA_Q_W)
        xs = _mix(o_s, z_s, ob_s, gates_s, xs, mod_s, 1, gn_row, wb, wo)
        outs["gdn_s"].append(s_s)
        outs["conv_s"].append(jnp.concatenate([state_conv[l], qkv_s3], axis=1)[:, -(CONV_W - 1):])
        kv_s3 = kv_s.reshape(bs, ls, 2 * SWA_KV_W)
        knew = jnp.concatenate([ck, kv_s3[..., :SWA_KV_W]], axis=1)[:, -WINDOW:]
        vnew = jnp.concatenate([cv, kv_s3[..., SWA_KV_W:]], axis=1)[:, -WINDOW:]
        outs["k_s"].append(knew.reshape(bs, WINDOW, SWA_KV_HEADS, SWA_HEAD_DIM))
        outs["v_s"].append(vnew.reshape(bs, WINDOW, SWA_KV_HEADS, SWA_HEAD_DIM))

        i = l // 2
        if l % 2 == 0:
            wg, wu, wd = (w_gate_dense[i].astype(BF16), w_up_dense[i].astype(BF16),
                          w_down_dense[i].astype(BF16))
            xp = _ffn(xp, mod_p, tpm_p, nffn, wg, wu, wd)
            xs = _ffn(xs, mod_s.reshape(ts // (2 * tm), 2 * tm, 6 * D_MODEL), 1, nffn, wg, wu, wd)
        else:
            wr_pad = jnp.concatenate([w_router[i], jnp.zeros((D_MODEL, LANES - N_EXPERTS), F32)], axis=1)
            h_p, idx_p, wt_p, rank_p, cnt_p = _route(xp, mod_p, tpm_p, nffn, wr_pad,
                                                     jnp.zeros((SUBLANES, LANES), F32))
            h_s, idx_s, wt_s, rank_s, cnt = _route(xs, mod_s, 1, nffn, wr_pad, cnt_p)
            top_i = jnp.concatenate([idx_p[:, :TOP_K], idx_s[:, :TOP_K]], axis=0)
            rank = jnp.concatenate([rank_p[:, :TOP_K], rank_s[:, :TOP_K]], axis=0)
            pos, p_rows, tile_expert, n_used = _dispatch(top_i, rank, cnt[0, :N_EXPERTS].astype(jnp.int32))
            n_assign = (tp + ts) * TOP_K
            pos_flat = pos.reshape(n_assign)
            tok = jnp.arange(n_assign, dtype=jnp.int32) // TOP_K
            row_token = jnp.zeros((p_rows,), jnp.int32).at[pos_flat].set(tok)
            xg = jnp.take(jnp.concatenate([h_p, h_s], axis=0), row_token, axis=0)
            y = _moe(xg, tile_expert, n_used, w_gate_moe[i], w_up_moe[i], w_down_moe[i])
            yg = jnp.take(y, pos_flat, axis=0).reshape(tp + ts, TOP_K * D_MODEL)
            fin = final_norm.reshape(1, D_MODEL)
            assert l == DEPTH - 1
            xp = _combine(xp, yg, 0, wt_p, mod_p, tpm_p, fin)
            xs = _combine(xs, yg, tp, wt_s, mod_s, 1, fin)

    st = lambda k: jnp.stack(outs[k])
    return (xp.reshape(bp, lp, D_MODEL), xs.reshape(bs, ls, D_MODEL),
            st("gdn_p"), st("conv_p"), st("k_p"), st("v_p"),
            st("gdn_s"), st("conv_s"), st("k_s"), st("v_s"))
```

```python
import functools

import jax
import jax.numpy as jnp
from jax import lax
from jax.experimental import pallas as pl
from jax.experimental.pallas import tpu as pltpu

F32 = jnp.float32
BF16 = jnp.bfloat16

D_MODEL = 1024
DEPTH = 2
GDN_HEADS = 4
GDN_DK = 128
GDN_DV = 128
CONV_W = 4
SWA_HEADS = 8
SWA_KV_HEADS = 2
SWA_HEAD_DIM = 64
SWA_GROUP = SWA_HEADS // SWA_KV_HEADS
WINDOW = 128
GDN_QK_W = GDN_HEADS * GDN_DK
GDN_V_W = GDN_HEADS * GDN_DV
CONV_DIM = 2 * GDN_QK_W + GDN_V_W
SWA_Q_W = SWA_HEADS * SWA_HEAD_DIM
SWA_KV_W = SWA_KV_HEADS * SWA_HEAD_DIM
N_EXPERTS = 8
TOP_K = 2
EPS = 1e-6

LANES = 128
SUBLANES = 8
VMEM_LIMIT = 56 * 1024 * 1024
NEG_BIG = -1e30

GDN_CHUNK = 64
TOKEN_TILE = 256
MOE_TILE = 1024
MOE_FF_TILE = 512
FFN_FF_TILE = 1408

IN_SPLITS = (CONV_DIM, GDN_V_W, SWA_Q_W, 2 * SWA_KV_W, 2 * D_MODEL, LANES)
IN_PAD_W = sum(IN_SPLITS)


def _silu(x):
    return x * jax.nn.sigmoid(x)


def _softplus(x):
    return jnp.maximum(x, 0.0) + jnp.log1p(jnp.exp(-jnp.abs(x)))


def _dot(a, b):
    return jnp.dot(a, b, preferred_element_type=F32)


def _dot_nt(a, b):
    return lax.dot_general(a, b, (((1,), (1,)), ((), ())), preferred_element_type=F32)


def _dot_tn(a, b):
    return lax.dot_general(a, b, (((0,), (0,)), ((), ())), preferred_element_type=F32)


def _split2(a):
    hi = a.astype(BF16)
    lo = (a - hi.astype(F32)).astype(BF16)
    return hi, lo


def _split3(a):
    hi = a.astype(BF16)
    r = a - hi.astype(F32)
    mid = r.astype(BF16)
    lo = (r - mid.astype(F32)).astype(BF16)
    return hi, mid, lo


def _dot3(a, b):
    ah, al = _split2(a)
    bh, bl = _split2(b)
    return _dot(ah, bh) + (_dot(ah, bl) + _dot(al, bh))


def _dot_exact_lhs(a01, b):
    a = a01.astype(BF16)
    bh, bm, bl = _split3(b)
    return _dot(a, bh) + (_dot(a, bm) + _dot(a, bl))


def _rms(x):
    return x * lax.rsqrt(jnp.mean(x * x, axis=-1, keepdims=True) + EPS)


def _params(sem):
    return pltpu.CompilerParams(dimension_semantics=sem, vmem_limit_bytes=VMEM_LIMIT)


def _ada_kernel(c_ref, w_ref, b_ref, o_ref):
    o_ref[...] = _dot3(_silu(c_ref[...]), w_ref[...]) + b_ref[...]


def _ada(c_all, w_ada, b_ada):
    m = c_all.shape[0]
    tn = 1536
    n = 6 * D_MODEL
    return pl.pallas_call(
        _ada_kernel,
        grid=(DEPTH, n // tn),
        in_specs=[
            pl.BlockSpec((m, D_MODEL), lambda l, j: (0, 0)),
            pl.BlockSpec((None, D_MODEL, tn), lambda l, j: (l, 0, j)),
            pl.BlockSpec((None, 1, tn), lambda l, j: (l, 0, j)),
        ],
        out_specs=pl.BlockSpec((None, m, tn), lambda l, j: (l, 0, j)),
        out_shape=jax.ShapeDtypeStruct((DEPTH, m, n), F32),
        compiler_params=_params(("arbitrary", "arbitrary")),
        name="ada",
    )(c_all, w_ada, b_ada.reshape(DEPTH, 1, n))


def _mod_spec(mod, chunk, tiles_per_mod):
    r = mod.shape[1]
    return pl.BlockSpec((None, r, D_MODEL), lambda i: (i // tiles_per_mod, 0, chunk))


def _in_kernel(x_ref, sh_ref, sc_ref, g_ref, w_ref, *out_refs):
    h = _rms(x_ref[...]) * g_ref[...]
    h = (h * (1.0 + sc_ref[...]) + sh_ref[...]).astype(BF16)
    off = 0
    for ref in out_refs:
        n = ref.shape[-1]
        ref[...] = _dot(h, w_ref[:, off:off + n])
        off += n


def _in_proj(x, mod, tiles_per_mod, norm_g, w_p):
    t = x.shape[0]
    tm = TOKEN_TILE
    return pl.pallas_call(
        _in_kernel,
        grid=(t // tm,),
        in_specs=[
            pl.BlockSpec((tm, D_MODEL), lambda i: (i, 0)),
            _mod_spec(mod, 0, tiles_per_mod),
            _mod_spec(mod, 1, tiles_per_mod),
            pl.BlockSpec((1, D_MODEL), lambda i: (0, 0)),
            pl.BlockSpec((D_MODEL, IN_PAD_W), lambda i: (0, 0)),
        ],
        out_specs=[pl.BlockSpec((tm, n), lambda i: (i, 0)) for n in IN_SPLITS],
        out_shape=[jax.ShapeDtypeStruct((t, n), F32) for n in IN_SPLITS],
        compiler_params=_params(("arbitrary",)),
        name="in_proj",
    )(x, mod, mod, norm_g, w_p)


def _inv_unit_lower(lms, nfac):
    c = lms[0].shape[0]
    r = lax.broadcasted_iota(jnp.int32, (c, c), 0)
    q = lax.broadcasted_iota(jnp.int32, (c, c), 1)
    eye = jnp.where(r == q, 1.0, 0.0)
    ts = [eye - lm for lm in lms]
    ps = list(lms)
    for _ in range(nfac - 1):
        pbs = [p.astype(BF16) for p in ps]
        ps = [_dot(pb, pb) for pb in pbs]
        ts = [t + _dot(t.astype(BF16), p.astype(BF16)) for t, p in zip(ts, ps)]
    splits = [(_split2(t), _split2(lm)) for t, lm in zip(ts, lms)]
    es = [(eye - t) - (_dot(lh, th) + (_dot(lh, tl) + _dot(ll, th)))
          for t, ((th, tl), (lh, ll)) in zip(ts, splits)]
    return [t + _dot(th, e.astype(BF16)) for t, ((th, _), _), e in zip(ts, splits, es)]


GDN_STACK = GDN_HEADS * GDN_CHUNK
GDN_PREP_CHUNKS = 4


def _gdn_prep_kernel(qkv_ref, hist_ref, ab_ref, cw_ref, alog_ref, dtb_ref,
                     u_ref, wq_ref, kdt_ref, qk_ref, egl_ref, xs_ref,
                     *, blk, lv, nfac, chunks_per_seq):
    c = GDN_CHUNK
    hc = GDN_STACK
    cb = GDN_PREP_CHUNKS
    hist = SUBLANES
    sh = blk.bit_length() - 1
    first = ((pl.program_id(0) * cb) % chunks_per_seq) == 0

    @pl.when(first)
    def _():
        xs_ref[0:hist, :] = jnp.zeros((hist, CONV_DIM), F32)

    @pl.when(jnp.logical_not(first))
    def _():
        xs_ref[0:hist, :] = hist_ref[...]

    xs_ref[hist:hist + cb * c, :] = qkv_ref[...]

    ri = lax.broadcasted_iota(jnp.int32, (hc, hc), 0)
    cj = lax.broadcasted_iota(jnp.int32, (hc, hc), 1)
    same = (ri >> sh) == (cj >> sh)
    incl = same & (ri >= cj)
    strict = same & (ri > cj)
    rt = lax.broadcasted_iota(jnp.int32, (c, c), 0)
    ct = lax.broadcasted_iota(jnp.int32, (c, c), 1)
    same_t = (rt >> sh) == (ct >> sh)
    cum_t = jnp.where(same_t & (rt >= ct), 1.0, 0.0)
    tot_t = jnp.where(same_t, 1.0, 0.0)
    if lv < blk:
        rvalid = (lax.broadcasted_iota(jnp.int32, (c, 1), 0) & (blk - 1)) >= (blk - lv)

    def stack_cols(a, lane0):
        return jnp.concatenate([a[:, lane0 + h:lane0 + h + 1] for h in range(GDN_HEADS)], axis=0)

    lms, rhs, qds, gtots = [], [], [], []
    for ci in range(cb):
        base = hist + ci * c

        def conv(c0, base=base):
            y = xs_ref[base - 3:base - 3 + c, c0:c0 + LANES] * cw_ref[0:1, c0:c0 + LANES]
            for i in range(1, CONV_W):
                y = y + xs_ref[base - 3 + i:base - 3 + i + c, c0:c0 + LANES] * cw_ref[i:i + 1, c0:c0 + LANES]
            y = _silu(y)
            return jnp.where(rvalid, y, 0.0) if lv < blk else y

        ab = ab_ref[ci * c:(ci + 1) * c, :]
        g_all = -jnp.exp(alog_ref[...]) * _softplus(ab + dtb_ref[...])
        beta_all = jax.nn.sigmoid(ab)
        if lv < blk:
            g_all = jnp.where(rvalid, g_all, 0.0)
            beta_all = jnp.where(rvalid, beta_all, 0.0)
        gcum = _dot_exact_lhs(cum_t, g_all)
        gtot = _dot_exact_lhs(tot_t, g_all)
        gc = stack_cols(gcum, 0)
        gl = stack_cols(gtot, 0)
        beta = stack_cols(beta_all, GDN_HEADS)
        gcb = jnp.broadcast_to(gc, (hc, LANES))
        gr = jnp.concatenate([gcb[0:LANES].T[0:1], gcb[LANES:].T[0:1]], axis=1)

        q = jnp.concatenate([conv(h * GDN_DK) for h in range(GDN_HEADS)], axis=0)
        k = jnp.concatenate([conv(GDN_QK_W + h * GDN_DK) for h in range(GDN_HEADS)], axis=0)
        v = jnp.concatenate([conv(2 * GDN_QK_W + h * GDN_DV) for h in range(GDN_HEADS)], axis=0)
        q = q * lax.rsqrt(jnp.sum(q * q, axis=-1, keepdims=True) + EPS) * (GDN_DK ** -0.5)
        k = k * lax.rsqrt(jnp.sum(k * k, axis=-1, keepdims=True) + EPS)

        decay = jnp.where(incl, jnp.exp(jnp.minimum(gc - gr, 0.0)), 0.0)
        kb = k * beta
        k16 = k.astype(BF16)
        lms.append(jnp.where(strict, _dot_nt(kb.astype(BF16), k16) * decay, 0.0))
        eg = jnp.exp(gc)
        rhs.append(_split2(jnp.concatenate([v * beta, kb * eg], axis=1)))
        qds.append(q * eg)
        gtots.append(gtot)
        kd = k * jnp.exp(gl - gc)
        qk_ref[ci] = (_dot_nt(q.astype(BF16), k16) * decay).astype(BF16)
        kdt_ref[ci] = kd.T.astype(BF16)

    tinvs = _inv_unit_lower(lms, nfac)
    for ci in range(cb):
        t16 = tinvs[ci].astype(BF16)
        rh, rl = rhs[ci]
        x = _dot(t16, rh) + _dot(t16, rl)
        w = x[:, GDN_DV:]
        for h in range(GDN_HEADS):
            sl = slice(h * c, (h + 1) * c)
            u_ref[ci * c:(ci + 1) * c, h * GDN_DV:(h + 1) * GDN_DV] = x[sl, :GDN_DV]
            wq_ref[ci, h] = jnp.concatenate([w[sl], qds[ci][sl]], axis=0).astype(BF16)
            egl = jnp.exp(jnp.broadcast_to(gtots[ci][:, h:h + 1], (c, LANES)))
            egl_ref[ci, h] = egl.reshape(c // SUBLANES, SUBLANES, LANES)[:, 0, :]


def _gdn_scan_kernel(u_ref, wq_ref, kdt_ref, qk_ref, egl_ref, s0_ref, o_ref, s_ref, *, blk, sb):
    c = GDN_CHUNK
    nb = c // blk
    sh = blk.bit_length() - 1
    n = pl.program_id(1)

    @pl.when(n == 0)
    def _():
        s_ref[...] = s0_ref[...]

    rblk2 = (lax.broadcasted_iota(jnp.int32, (2 * c, 1), 0) & (c - 1)) >> sh
    rblk_s = (lax.broadcasted_iota(jnp.int32, (GDN_STACK, 1), 0) & (c - 1)) >> sh
    rrs = []
    for b in range(sb):
        for h in range(GDN_HEADS):
            wq = wq_ref[b, h]
            rr = _dot(wq, s_ref[b, 0, h].astype(BF16))
            for j in range(1, nb):
                rr = jnp.where(rblk2 == j, _dot(wq, s_ref[b, j, h].astype(BF16)), rr)
            rrs.append(rr)
    vnews = []
    for b in range(sb):
        rb = rrs[b * GDN_HEADS:(b + 1) * GDN_HEADS]
        vnew = jnp.concatenate([u_ref[b, :, h * GDN_DV:(h + 1) * GDN_DV] - rb[h][:c]
                                for h in range(GDN_HEADS)], axis=0)
        vnews.append(vnew)
        o = jnp.concatenate([r[c:] for r in rb], axis=0) + _dot(qk_ref[b], vnew.astype(BF16))
        for h in range(GDN_HEADS):
            o_ref[b, :, h * GDN_DV:(h + 1) * GDN_DV] = o[h * c:(h + 1) * c]
    for b in range(sb):
        vnew = vnews[b]
        for j in range(nb):
            vj = (vnew if nb == 1 else jnp.where(rblk_s == j, vnew, 0.0)).astype(BF16)
            zero = jnp.zeros((c, GDN_DV), BF16)
            vbd = jnp.concatenate(
                [jnp.concatenate([vj[h * c:(h + 1) * c] if g == h else zero for g in range(GDN_HEADS)], axis=1)
                 for h in range(GDN_HEADS)], axis=0)
            upd = _dot(kdt_ref[b], vbd)
            for h in range(GDN_HEADS):
                s_ref[b, j, h] = (egl_ref[b, h, j:j + 1, :] * s_ref[b, j, h]
                                  + upd[:, h * GDN_DV:(h + 1) * GDN_DV])


def _gdn(qkv, ab, s0, conv_w, alog_row, dtb_row, *, blk, lv, chunks_per_seq, sb):
    rows = qkv.shape[0]
    c = GDN_CHUNK
    hc = GDN_STACK
    cb = GDN_PREP_CHUNKS
    nb = c // blk
    assert blk in (SUBLANES, c) and lv <= blk
    nfac = max(1, (lv - 1).bit_length())
    nch = rows // c
    nseq = nch // chunks_per_seq
    assert nseq * nb == s0.shape[0] and nch % cb == 0
    assert chunks_per_seq == 1 or chunks_per_seq % cb == 0
    hist_blocks = cb * c // SUBLANES
    prep = functools.partial(_gdn_prep_kernel, blk=blk, lv=lv, nfac=nfac, chunks_per_seq=chunks_per_seq)
    u, wq, kdt, qk, egl = pl.pallas_call(
        prep,
        grid=(nch // cb,),
        in_specs=[
            pl.BlockSpec((cb * c, CONV_DIM), lambda i: (i, 0)),
            pl.BlockSpec((SUBLANES, CONV_DIM), lambda i: (jnp.maximum(i * hist_blocks - 1, 0), 0)),
            pl.BlockSpec((cb * c, LANES), lambda i: (i, 0)),
            pl.BlockSpec((CONV_W, CONV_DIM), lambda i: (0, 0)),
            pl.BlockSpec((1, LANES), lambda i: (0, 0)),
            pl.BlockSpec((1, LANES), lambda i: (0, 0)),
        ],
        out_specs=[
            pl.BlockSpec((cb * c, GDN_V_W), lambda i: (i, 0)),
            pl.BlockSpec((cb, GDN_HEADS, 2 * c, GDN_DK), lambda i: (i, 0, 0, 0)),
            pl.BlockSpec((cb, GDN_DK, hc), lambda i: (i, 0, 0)),
            pl.BlockSpec((cb, hc, hc), lambda i: (i, 0, 0)),
            pl.BlockSpec((cb, GDN_HEADS, SUBLANES, LANES), lambda i: (i, 0, 0, 0)),
        ],
        out_shape=[
            jax.ShapeDtypeStruct((rows, GDN_V_W), F32),
            jax.ShapeDtypeStruct((nch, GDN_HEADS, 2 * c, GDN_DK), BF16),
            jax.ShapeDtypeStruct((nch, GDN_DK, hc), BF16),
            jax.ShapeDtypeStruct((nch, hc, hc), BF16),
            jax.ShapeDtypeStruct((nch, GDN_HEADS, SUBLANES, LANES), F32),
        ],
        scratch_shapes=[pltpu.VMEM((SUBLANES + cb * c, CONV_DIM), F32)],
        compiler_params=_params(("arbitrary",)),
        name="gdn_prep",
    )(qkv, qkv, ab, conv_w, alog_row, dtb_row)

    n = chunks_per_seq
    scan = functools.partial(_gdn_scan_kernel, blk=blk, sb=sb)
    seq5 = lambda a: a.reshape((nseq, n) + a.shape[1:])
    s0g = s0.reshape(nseq, nb, GDN_HEADS, GDN_DK, GDN_DV)
    o, s_new = pl.pallas_call(
        scan,
        grid=(nseq // sb, n),
        in_specs=[
            pl.BlockSpec((sb, c, GDN_V_W), lambda i, t: (i, t, 0)),
            pl.BlockSpec((sb, None, GDN_HEADS, 2 * c, GDN_DK), lambda i, t: (i, t, 0, 0, 0)),
            pl.BlockSpec((sb, None, GDN_DK, hc), lambda i, t: (i, t, 0, 0)),
            pl.BlockSpec((sb, None, hc, hc), lambda i, t: (i, t, 0, 0)),
            pl.BlockSpec((sb, None, GDN_HEADS, SUBLANES, LANES), lambda i, t: (i, t, 0, 0, 0)),
            pl.BlockSpec((sb, nb, GDN_HEADS, GDN_DK, GDN_DV), lambda i, t: (i, 0, 0, 0, 0)),
        ],
        out_specs=[
            pl.BlockSpec((sb, c, GDN_V_W), lambda i, t: (i, t, 0)),
            pl.BlockSpec((sb, nb, GDN_HEADS, GDN_DK, GDN_DV), lambda i, t: (i, 0, 0, 0, 0)),
        ],
        out_shape=[
            jax.ShapeDtypeStruct((nseq, n * c, GDN_V_W), F32),
            jax.ShapeDtypeStruct(s0g.shape, F32),
        ],
        compiler_params=_params(("arbitrary", "arbitrary")),
        name="gdn_scan",
    )(u.reshape(nseq, n * c, GDN_V_W), seq5(wq), seq5(kdt), seq5(qk), seq5(egl), s0g)
    return o.reshape(rows, GDN_V_W), s_new.reshape(s0.shape)


def _swa_kernel(q_ref, kc_ref, vc_ref, kp_ref, vp_ref, sink_ref, o_ref, *, qb, bb, prev_is_cache):
    w = WINDOW
    n = pl.program_id(1)
    rows = SWA_GROUP * qb
    qi = lax.broadcasted_iota(jnp.int32, (rows, 2 * w), 0) & (qb - 1)
    kj = lax.broadcasted_iota(jnp.int32, (rows, 2 * w), 1)
    ok = ((kj < w) & (kj >= qi)) | ((kj >= w) & (kj - w <= qi))
    if not prev_is_cache:
        ok = ok & ((kj >= w) | (n > 0))
    scale = SWA_HEAD_DIM ** -0.5

    def pad_rows(a):
        return a if qb == w else jnp.concatenate([a, jnp.zeros((w - qb, a.shape[1]), a.dtype)], axis=0)

    for b in range(bb):
        r0 = b * qb
        kc_all = pad_rows(kc_ref[r0:r0 + qb, :])
        vc_all = pad_rows(vc_ref[r0:r0 + qb, :])
        kp_all = kp_ref[b] if prev_is_cache else kp_ref[...]
        vp_all = vp_ref[b] if prev_is_cache else vp_ref[...]
        ss, sinks, vs = [], [], []
        ones = jnp.ones((2 * w, SWA_HEAD_DIM), BF16)
        for hk in range(SWA_KV_HEADS):
            c0 = hk * SWA_HEAD_DIM
            sl = slice(c0, c0 + SWA_HEAD_DIM)
            k_all = jnp.concatenate([kp_all[:, sl], kc_all[:, sl]], axis=0).astype(BF16)
            v_all = jnp.concatenate([vp_all[:, sl], vc_all[:, sl]], axis=0).astype(BF16)
            vs.append(jnp.concatenate([v_all, ones], axis=1))
            heads = [hk * SWA_GROUP + g for g in range(SWA_GROUP)]
            q = jnp.concatenate(
                [q_ref[r0:r0 + qb, hd * SWA_HEAD_DIM:(hd + 1) * SWA_HEAD_DIM] for hd in heads],
                axis=0).astype(BF16)
            sinks.append(jnp.concatenate(
                [jnp.broadcast_to(sink_ref[0:1, hd:hd + 1], (qb, 1)) for hd in heads], axis=0))
            ss.append(jnp.where(ok, _dot_nt(q, k_all) * scale, NEG_BIG))
        ms = [jnp.maximum(jnp.max(s, axis=-1, keepdims=True), sink) for s, sink in zip(ss, sinks)]
        accs = [_dot(jnp.exp(s - m).astype(BF16), v) for s, m, v in zip(ss, ms, vs)]
        for hk in range(SWA_KV_HEADS):
            acc = accs[hk]
            den = acc[:, SWA_HEAD_DIM:SWA_HEAD_DIM + 1] + jnp.exp(sinks[hk] - ms[hk])
            res = acc[:, :SWA_HEAD_DIM] / den
            for g in range(SWA_GROUP):
                hd = hk * SWA_GROUP + g
                o_ref[r0:r0 + qb, hd * SWA_HEAD_DIM:(hd + 1) * SWA_HEAD_DIM] = res[g * qb:(g + 1) * qb]


def _swa(q, kv, sink_row, *, qb, bb, blocks_per_seq, cache_k=None, cache_v=None):
    rows = q.shape[0]
    w = WINDOW
    prev_is_cache = cache_k is not None
    steps = rows // (qb * bb)
    nseq = steps // blocks_per_seq
    kern = functools.partial(_swa_kernel, qb=qb, bb=bb, prev_is_cache=prev_is_cache)
    cur = lambda i, n: (i * blocks_per_seq + n, 0)
    if prev_is_cache:
        kp_spec = pl.BlockSpec((bb, w, SWA_KV_W), lambda i, n: (i, 0, 0))
        vp_spec = kp_spec
        kp_arr, vp_arr = cache_k, cache_v
    else:
        assert qb == w and bb == 1
        kp_spec = pl.BlockSpec((w, SWA_KV_W), lambda i, n: (i * blocks_per_seq + jnp.maximum(n - 1, 0), 0))
        vp_spec = pl.BlockSpec((w, SWA_KV_W), lambda i, n: (i * blocks_per_seq + jnp.maximum(n - 1, 0), 1))
        kp_arr, vp_arr = kv, kv
    return pl.pallas_call(
        kern,
        grid=(nseq, blocks_per_seq),
        in_specs=[
            pl.BlockSpec((qb * bb, SWA_Q_W), cur),
            pl.BlockSpec((qb * bb, SWA_KV_W), cur),
            pl.BlockSpec((qb * bb, SWA_KV_W), lambda i, n: (i * blocks_per_seq + n, 1)),
            kp_spec,
            vp_spec,
            pl.BlockSpec((1, LANES), lambda i, n: (0, 0)),
        ],
        out_specs=pl.BlockSpec((qb * bb, SWA_Q_W), cur),
        out_shape=jax.ShapeDtypeStruct((rows, SWA_Q_W), F32),
        compiler_params=_params(("arbitrary", "arbitrary")),
        name="swa",
    )(q, kv, kv, kp_arr, vp_arr, sink_row)


def _mix_kernel(o_ref, z_ref, ob_ref, gates_ref, x_ref, gt_ref, gn_ref, wb_ref, wo_ref, y_ref):
    parts = []
    for h in range(GDN_HEADS):
        sl = slice(h * GDN_DV, (h + 1) * GDN_DV)
        parts.append(_rms(o_ref[:, sl]) * gn_ref[...] * _silu(z_ref[:, sl]))
    oa = jnp.concatenate(parts, axis=1).astype(BF16)
    pa = _dot(oa, wb_ref[0:GDN_V_W, :])
    pb = _dot(ob_ref[...].astype(BF16), wb_ref[GDN_V_W:, :])
    mixed = (jax.nn.sigmoid(gates_ref[:, 0:D_MODEL]) * pa
             + jax.nn.sigmoid(gates_ref[:, D_MODEL:]) * pb)
    y_ref[...] = x_ref[...] + gt_ref[...] * _dot(mixed.astype(BF16), wo_ref[...])


def _mix(o_raw, z, ob, gates, x, mod, tiles_per_mod, gn_row, wb, wo):
    t = x.shape[0]
    tm = TOKEN_TILE
    row = lambda n: pl.BlockSpec((tm, n), lambda i: (i, 0))
    full = lambda a: pl.BlockSpec(a.shape, lambda i: (0, 0))
    return pl.pallas_call(
        _mix_kernel,
        grid=(t // tm,),
        in_specs=[row(GDN_V_W), row(GDN_V_W), row(SWA_Q_W), row(2 * D_MODEL), row(D_MODEL),
                  _mod_spec(mod, 2, tiles_per_mod), full(gn_row), full(wb), full(wo)],
        out_specs=row(D_MODEL),
        out_shape=jax.ShapeDtypeStruct((t, D_MODEL), F32),
        compiler_params=_params(("arbitrary",)),
        name="mix",
    )(o_raw, z, ob, gates, x, mod, gn_row, wb, wo)


def _ffn_kernel(x_ref, sh_ref, sc_ref, gt_ref, g_ref, wg_ref, wu_ref, wd_ref, y_ref, h_ref, acc_ref):
    j = pl.program_id(1)

    @pl.when(j == 0)
    def _():
        h = _rms(x_ref[...]) * g_ref[...]
        h_ref[...] = (h * (1.0 + sc_ref[...]) + sh_ref[...]).astype(BF16)

    h = h_ref[...]
    a = (_silu(_dot(h, wg_ref[...])) * _dot(h, wu_ref[...])).astype(BF16)
    d = _dot(a, wd_ref[...])

    @pl.when(j == 0)
    def _():
        acc_ref[...] = d

    @pl.when(j > 0)
    def _():
        acc_ref[...] += d

    @pl.when(j == pl.num_programs(1) - 1)
    def _():
        y_ref[...] = x_ref[...] + gt_ref[...] * acc_ref[...]


def _ffn(x, mod, tiles_per_mod, norm_g, wg, wu, wd):
    t = x.shape[0]
    tm = 2 * TOKEN_TILE
    tpm = max(1, tiles_per_mod // 2)
    d_ff = wg.shape[1]
    tf = FFN_FF_TILE
    r = mod.shape[1]
    mspec = lambda chunk: pl.BlockSpec((None, r, D_MODEL), lambda i, j: (i // tpm, 0, chunk))
    return pl.pallas_call(
        _ffn_kernel,
        grid=(t // tm, d_ff // tf),
        in_specs=[
            pl.BlockSpec((tm, D_MODEL), lambda i, j: (i, 0)),
            mspec(3), mspec(4), mspec(5),
            pl.BlockSpec((1, D_MODEL), lambda i, j: (0, 0)),
            pl.BlockSpec((D_MODEL, tf), lambda i, j: (0, j)),
            pl.BlockSpec((D_MODEL, tf), lambda i, j: (0, j)),
            pl.BlockSpec((tf, D_MODEL), lambda i, j: (j, 0)),
        ],
        out_specs=pl.BlockSpec((tm, D_MODEL), lambda i, j: (i, 0)),
        out_shape=jax.ShapeDtypeStruct((t, D_MODEL), F32),
        scratch_shapes=[pltpu.VMEM((tm, D_MODEL), BF16), pltpu.VMEM((tm, D_MODEL), F32)],
        compiler_params=_params(("arbitrary", "arbitrary")),
        name="ffn",
    )(x, mod, mod, mod, norm_g, wg, wu, wd)


def _route_kernel(x_ref, sh_ref, sc_ref, g_ref, wr_ref, h_ref, idx_ref, wt_ref):
    h = _rms(x_ref[...]) * g_ref[...]
    h = h * (1.0 + sc_ref[...]) + sh_ref[...]
    h_ref[...] = h.astype(BF16)
    logits = _dot3(h, wr_ref[...])
    lane_i = lax.broadcasted_iota(jnp.int32, logits.shape, 1)
    lane = lane_i.astype(F32)
    logits = jnp.where(lane_i < N_EXPERTS, logits, NEG_BIG)
    m1 = jnp.max(logits, axis=-1, keepdims=True)
    i1 = jnp.min(jnp.where(logits == m1, lane, float(LANES)), axis=-1, keepdims=True)
    rest = jnp.where(lane == i1, NEG_BIG, logits)
    m2 = jnp.max(rest, axis=-1, keepdims=True)
    i2 = jnp.min(jnp.where(rest == m2, lane, float(LANES)), axis=-1, keepdims=True)
    e = jnp.exp(m2 - m1)
    den = 1.0 + e
    idx_ref[...] = jnp.where(lane_i == 0, i1, jnp.where(lane_i == 1, i2, 0.0)).astype(jnp.int32)
    wt_ref[...] = jnp.where(lane_i == 0, 1.0 / den, jnp.where(lane_i == 1, e / den, 0.0))


def _route(x, mod, tiles_per_mod, norm_g, wr_pad):
    t = x.shape[0]
    tm = TOKEN_TILE
    row = lambda n: pl.BlockSpec((tm, n), lambda i: (i, 0))
    return pl.pallas_call(
        _route_kernel,
        grid=(t // tm,),
        in_specs=[row(D_MODEL), _mod_spec(mod, 3, tiles_per_mod), _mod_spec(mod, 4, tiles_per_mod),
                  pl.BlockSpec((1, D_MODEL), lambda i: (0, 0)),
                  pl.BlockSpec((D_MODEL, LANES), lambda i: (0, 0))],
        out_specs=[row(D_MODEL), row(LANES), row(LANES)],
        out_shape=[jax.ShapeDtypeStruct((t, D_MODEL), BF16),
                   jax.ShapeDtypeStruct((t, LANES), jnp.int32),
                   jax.ShapeDtypeStruct((t, LANES), F32)],
        compiler_params=_params(("arbitrary",)),
        name="route",
    )(x, mod, mod, norm_g, wr_pad)


def _moe_kernel(te_ref, nu_ref, x_ref, wg_ref, wu_ref, wd_ref, o_ref):
    i = pl.program_id(0)
    j = pl.program_id(1)
    used = i < nu_ref[0]

    @pl.when(used)
    def _():
        x = x_ref[...]
        a = (_silu(_dot(x, wg_ref[...].astype(BF16))) * _dot(x, wu_ref[...].astype(BF16))).astype(BF16)
        d = _dot(a, wd_ref[...].astype(BF16))

        @pl.when(j == 0)
        def _():
            o_ref[...] = d

        @pl.when(j > 0)
        def _():
            o_ref[...] += d

    @pl.when(jnp.logical_not(used) & (j == 0))
    def _():
        o_ref[...] = jnp.zeros_like(o_ref)


def _moe(xs, tile_expert, n_used, wg, wu, wd):
    p = xs.shape[0]
    tm = MOE_TILE
    tf = MOE_FF_TILE
    d_ff = wg.shape[2]
    nj = d_ff // tf

    def jj(i, j, nu):
        return jnp.where(i < nu[0], j, nj - 1)

    grid_spec = pltpu.PrefetchScalarGridSpec(
        num_scalar_prefetch=2,
        grid=(p // tm, nj),
        in_specs=[
            pl.BlockSpec((tm, D_MODEL), lambda i, j, te, nu: (i, 0)),
            pl.BlockSpec((None, D_MODEL, tf), lambda i, j, te, nu: (te[i], 0, jj(i, j, nu))),
            pl.BlockSpec((None, D_MODEL, tf), lambda i, j, te, nu: (te[i], 0, jj(i, j, nu))),
            pl.BlockSpec((None, tf, D_MODEL), lambda i, j, te, nu: (te[i], jj(i, j, nu), 0)),
        ],
        out_specs=pl.BlockSpec((tm, D_MODEL), lambda i, j, te, nu: (i, 0)),
    )
    return pl.pallas_call(
        _moe_kernel,
        grid_spec=grid_spec,
        out_shape=jax.ShapeDtypeStruct((p, D_MODEL), F32),
        compiler_params=_params(("arbitrary", "arbitrary")),
        name="moe",
    )(tile_expert, n_used, xs, wg, wu, wd)


def _dispatch(top_i):
    t = top_i.shape[0]
    tm = MOE_TILE
    n_assign = t * TOP_K
    n_tiles = -(-n_assign // tm) + N_EXPERTS
    e_flat = top_i.reshape(-1)
    onehot = (e_flat[:, None] == jnp.arange(N_EXPERTS, dtype=jnp.int32)[None, :]).astype(jnp.int32)
    csum = jnp.cumsum(onehot, axis=0)
    counts = csum[-1]
    tiles_e = (counts + tm - 1) // tm
    tile_end = jnp.cumsum(tiles_e)
    tile_start = tile_end - tiles_e
    pos = jnp.sum(onehot * (csum - 1 + tile_start[None, :] * tm), axis=1)
    n_used = tile_end[-1]
    tile_ids = jnp.arange(n_tiles, dtype=jnp.int32)
    tile_expert = jnp.sum((tile_ids[:, None] >= tile_end[None, :]).astype(jnp.int32), axis=1)
    last_e = jnp.max(jnp.where(tiles_e > 0, jnp.arange(N_EXPERTS, dtype=jnp.int32), 0))
    tile_expert = jnp.where(tile_ids < n_used, tile_expert, last_e).astype(jnp.int32)
    p = n_tiles * tm
    tok = jnp.arange(n_assign, dtype=jnp.int32) // TOP_K
    row_token = jnp.zeros((p,), jnp.int32).at[pos].set(tok)
    return pos.reshape(t, TOP_K), row_token, tile_expert, n_used.reshape(1).astype(jnp.int32)


def _combine_kernel(x_ref, ya_ref, yb_ref, wt_ref, gt_ref, g_ref, y_ref):
    f = wt_ref[:, 0:1] * ya_ref[...] + wt_ref[:, 1:2] * yb_ref[...]
    x = x_ref[...] + gt_ref[...] * f
    y_ref[...] = _rms(x) * g_ref[...]


def _combine(x, ya, yb, wt, mod, tiles_per_mod, norm_g):
    t = x.shape[0]
    tm = TOKEN_TILE
    row = pl.BlockSpec((tm, D_MODEL), lambda i: (i, 0))
    return pl.pallas_call(
        _combine_kernel,
        grid=(t // tm,),
        in_specs=[row, row, row, pl.BlockSpec((tm, LANES), lambda i: (i, 0)),
                  _mod_spec(mod, 5, tiles_per_mod), pl.BlockSpec((1, D_MODEL), lambda i: (0, 0))],
        out_specs=row,
        out_shape=jax.ShapeDtypeStruct((t, D_MODEL), F32),
        compiler_params=_params(("arbitrary",)),
        name="combine",
    )(x, ya, yb, wt, mod, norm_g)


def _lane_row(vals):
    return jnp.zeros((1, LANES), F32).at[0, :vals.shape[0]].set(vals.astype(F32))


def _permute_w_in(w):
    c = [CONV_DIM, CONV_DIM + GDN_V_W]
    a0 = c[1]
    q0 = a0 + 2 * GDN_HEADS
    k0 = q0 + SWA_Q_W
    g0 = k0 + 2 * SWA_KV_W
    cols = [w[:, :c[0]], w[:, c[0]:c[1]], w[:, q0:k0], w[:, k0:g0], w[:, g0:],
            w[:, a0:q0], jnp.zeros((D_MODEL, LANES - 2 * GDN_HEADS), w.dtype)]
    return jnp.concatenate(cols, axis=1).astype(BF16)


def kernel(x_prompt, x_sample, c_prompt, c_sample, state_gdn, state_conv, cache_win_k, cache_win_v, w_in, conv_w, a_log, dt_bias, gdn_norm, attn_sinks, w_branch, w_out, w_ada, b_ada, norm_mix, norm_ffn, w_gate_dense, w_up_dense, w_down_dense, w_router, w_gate_moe, w_up_moe, w_down_moe, final_norm):
    bp, lp, _ = x_prompt.shape
    bs, ls, _ = x_sample.shape
    tp, ts = bp * lp, bs * ls
    tm = TOKEN_TILE
    pad_rows = SUBLANES - ls

    m_pad = -(-(bp + bs) // SUBLANES) * SUBLANES
    c_all = jnp.concatenate([c_prompt, c_sample, jnp.zeros((m_pad - bp - bs, D_MODEL), F32)], axis=0)
    ada = _ada(c_all, w_ada, b_ada)

    xp = x_prompt.reshape(tp, D_MODEL)
    xs = x_sample.reshape(ts, D_MODEL)
    outs = {k: [] for k in ("gdn_p", "conv_p", "k_p", "v_p", "gdn_s", "conv_s", "k_s", "v_s")}
    for l in range(DEPTH):
        mod_p = ada[l, :bp].reshape(bp, 1, 6 * D_MODEL)
        mod_s = jnp.repeat(ada[l, bp:bp + bs], ls, axis=0).reshape(ts // tm, tm, 6 * D_MODEL)
        tpm_p = lp // tm
        w_p = _permute_w_in(w_in[l])
        nmix = norm_mix[l].reshape(1, D_MODEL)
        nffn = norm_ffn[l].reshape(1, D_MODEL)
        alog_row, dtb_row = _lane_row(a_log[l]), _lane_row(dt_bias[l])
        sink_row = _lane_row(attn_sinks[l])
        gn_row = gdn_norm[l].reshape(1, GDN_DV)
        wb = w_branch[l].astype(BF16)
        wo = w_out[l].astype(BF16)

        qkv_p, z_p, qb_p, kv_p, gates_p, ab_p = _in_proj(xp, mod_p, tpm_p, nmix, w_p)
        s0_p = jnp.zeros((bp, GDN_HEADS, GDN_DK, GDN_DV), F32)
        o_p, s_p = _gdn(qkv_p, ab_p, s0_p, conv_w[l], alog_row, dtb_row,
                        blk=GDN_CHUNK, lv=GDN_CHUNK, chunks_per_seq=lp // GDN_CHUNK, sb=bp)
        ob_p = _swa(qb_p, kv_p, sink_row, qb=WINDOW, bb=1, blocks_per_seq=lp // WINDOW)
        xp = _mix(o_p, z_p, ob_p, gates_p, xp, mod_p, tpm_p, gn_row, wb, wo)
        outs["gdn_p"].append(s_p)
        outs["conv_p"].append(qkv_p.reshape(bp, lp, CONV_DIM)[:, lp - (CONV_W - 1):])
        kv3 = kv_p.reshape(bp, lp, 2 * SWA_KV_W)[:, lp - WINDOW:]
        outs["k_p"].append(kv3[..., :SWA_KV_W].reshape(bp, WINDOW, SWA_KV_HEADS, SWA_HEAD_DIM))
        outs["v_p"].append(kv3[..., SWA_KV_W:].reshape(bp, WINDOW, SWA_KV_HEADS, SWA_HEAD_DIM))

        qkv_s, z_s, qb_s, kv_s, gates_s, ab_s = _in_proj(xs, mod_s, 1, nmix, w_p)
        qkv_s3 = qkv_s.reshape(bs, ls, CONV_DIM)
        hist = jnp.concatenate([jnp.zeros((bs, SUBLANES - ls - (CONV_W - 1), CONV_DIM), F32),
                                state_conv[l], qkv_s3], axis=1)
        ab_blk = jnp.concatenate([jnp.zeros((bs, pad_rows, LANES), F32), ab_s.reshape(bs, ls, LANES)], axis=1)
        o_s, s_s = _gdn(hist.reshape(bs * SUBLANES, CONV_DIM), ab_blk.reshape(bs * SUBLANES, LANES),
                        state_gdn[l], conv_w[l], alog_row, dtb_row,
                        blk=SUBLANES, lv=ls, chunks_per_seq=1, sb=2)
        o_s = o_s.reshape(bs, SUBLANES, GDN_V_W)[:, pad_rows:].reshape(ts, GDN_V_W)
        pad_after = lambda a: jnp.concatenate(
            [a.reshape(bs, ls, -1), jnp.zeros((bs, pad_rows, a.shape[-1]), F32)], axis=1).reshape(bs * SUBLANES, -1)
        ck = cache_win_k[l].reshape(bs, WINDOW, SWA_KV_W)
        cv = cache_win_v[l].reshape(bs, WINDOW, SWA_KV_W)
        ob_s = _swa(pad_after(qb_s), pad_after(kv_s), sink_row, qb=SUBLANES, bb=8, blocks_per_seq=1,
                    cache_k=ck, cache_v=cv)
        ob_s = ob_s.reshape(bs, SUBLANES, SWA_Q_W)[:, :ls].reshape(ts, SWA_Q_W)
        xs = _mix(o_s, z_s, ob_s, gates_s, xs, mod_s, 1, gn_row, wb, wo)
        outs["gdn_s"].append(s_s)
        outs["conv_s"].append(jnp.concatenate([state_conv[l], qkv_s3], axis=1)[:, -(CONV_W - 1):])
        kv_s3 = kv_s.reshape(bs, ls, 2 * SWA_KV_W)
        knew = jnp.concatenate([ck, kv_s3[..., :SWA_KV_W]], axis=1)[:, -WINDOW:]
        vnew = jnp.concatenate([cv, kv_s3[..., SWA_KV_W:]], axis=1)[:, -WINDOW:]
        outs["k_s"].append(knew.reshape(bs, WINDOW, SWA_KV_HEADS, SWA_HEAD_DIM))
        outs["v_s"].append(vnew.reshape(bs, WINDOW, SWA_KV_HEADS, SWA_HEAD_DIM))

        i = l // 2
        if l % 2 == 0:
            wg, wu, wd = (w_gate_dense[i].astype(BF16), w_up_dense[i].astype(BF16),
                          w_down_dense[i].astype(BF16))
            xp = _ffn(xp, mod_p, tpm_p, nffn, wg, wu, wd)
            xs = _ffn(xs, mod_s.reshape(ts // (2 * tm), 2 * tm, 6 * D_MODEL), 1, nffn, wg, wu, wd)
        else:
            wr_pad = jnp.concatenate([w_router[i], jnp.zeros((D_MODEL, LANES - N_EXPERTS), F32)], axis=1)
            h_p, idx_p, wt_p = _route(xp, mod_p, tpm_p, nffn, wr_pad)
            h_s, idx_s, wt_s = _route(xs, mod_s, 1, nffn, wr_pad)
            h_all = jnp.concatenate([h_p, h_s], axis=0)
            top_i = jnp.concatenate([idx_p[:, :TOP_K], idx_s[:, :TOP_K]], axis=0)
            pos, row_token, tile_expert, n_used = _dispatch(top_i)
            xg = jnp.take(h_all, row_token, axis=0)
            y = _moe(xg, tile_expert, n_used, w_gate_moe[i], w_up_moe[i], w_down_moe[i])
            ya = jnp.take(y, pos[:, 0], axis=0)
            yb = jnp.take(y, pos[:, 1], axis=0)
            fin = final_norm.reshape(1, D_MODEL)
            assert l == DEPTH - 1
            xp = _combine(xp, ya[:tp], yb[:tp], wt_p, mod_p, tpm_p, fin)
            xs = _combine(xs, ya[tp:], yb[tp:], wt_s, mod_s, 1, fin)

    st = lambda k: jnp.stack(outs[k])
    return (xp.reshape(bp, lp, D_MODEL), xs.reshape(bs, ls, D_MODEL),
            st("gdn_p"), st("conv_p"), st("k_p"), st("v_p"),
            st("gdn_s"), st("conv_s"), st("k_s"), st("v_s"))
```
